```python
import math
import jax, jax.numpy as jnp
from jax import lax
import numpy as np

D_MODEL = 1024
BATCH = 2
SEQ = 8192
DEPTH = 2
DEC_BATCH = 128
DEC_SEQ = 8
PAST_LEN = 2048
PAGE_SIZE = 128

DH = 64
H_A = 4
H_B = 8
H_C = 8
H_I = 4
D_IDX = 64
TOPK_MAX = 256
N_META = 16
BLOCK_Q = 128
META_PAD = BLOCK_Q - N_META
ROT_DIM = DH // 4
ROPE_THETA = 500000.0
N_EXPERTS = 32
TOP_K_EXPERTS = 4
D_FF = D_MODEL
SWIGLU_ALPHA = 1.702
SWIGLU_LIMIT = 7.0
LN_EPS = 1e-5
RMS_EPS = 1e-5
DEEPNORM_ALPHA = (2 * DEPTH) ** 0.25
DEEPNORM_BETA = (8 * DEPTH) ** -0.25
NEG_INF = -1e30
FORGET_BIAS = 3.0
W_A = H_A * 2 * DH
W_B = H_B * DH
W_C = H_C * DH
PROJ_SIZES = (W_A, W_A, W_A, W_B, W_B, W_B, H_B, W_C, W_C, W_C, H_I * D_IDX, D_IDX, H_I, 3 * D_MODEL)
N_IN = sum(PROJ_SIZES)

kernel_name = 'hybrid_diff_fox_dsa_moe_step'


def layer_norm(x, g, b):
    xf = x.astype(jnp.float32)
    mu = jnp.mean(xf, -1, keepdims=True)
    var = jnp.mean(jnp.square(xf - mu), -1, keepdims=True)
    return ((xf - mu) * lax.rsqrt(var + LN_EPS) * g + b).astype(x.dtype)


def rope(x, pos):
    half = ROT_DIM // 2
    inv = ROPE_THETA ** (-jnp.arange(half, dtype=jnp.float32) / half)
    ang = pos.astype(jnp.float32)[:, None] * inv
    shape = (ang.shape[0],) + (1,) * (x.ndim - 3) + (half,)
    cos = jnp.cos(ang).reshape(shape).astype(x.dtype)
    sin = jnp.sin(ang).reshape(shape).astype(x.dtype)
    x1, x2 = x[..., :half], x[..., half:ROT_DIM]
    return jnp.concatenate([x1 * cos - x2 * sin, x2 * cos + x1 * sin, x[..., ROT_DIM:]], axis=-1)


def project(x, pos, w_in, b_in):
    n, t, _ = x.shape
    p = jnp.einsum('ntd,dc->ntc', x, w_in) + b_in
    splits = [int(v) for v in np.cumsum(PROJ_SIZES)[:-1]]
    aq, ak, av, bq, bk, bv, bf, cq, ck, cv, iq, ik, iw, g = jnp.split(p, splits, axis=-1)
    return dict(
        aq=rope(aq.reshape(n, t, H_A, 2, DH), pos),
        ak=rope(ak.reshape(n, t, H_A, 2, DH), pos),
        av=av.reshape(n, t, H_A, 2 * DH),
        bq=bq.reshape(n, t, H_B, DH),
        bk=bk.reshape(n, t, H_B, DH),
        bv=bv.reshape(n, t, H_B, DH),
        logf=jax.nn.log_sigmoid(bf.astype(jnp.float32)),
        cq=rope(cq.reshape(n, t, H_C, DH), pos),
        ck=rope(ck.reshape(n, t, H_C, DH), pos),
        cv=cv.reshape(n, t, H_C, DH),
        iq=rope(iq.reshape(n, t, H_I, D_IDX), pos),
        ik=rope(ik, pos),
        iw=iw * (H_I ** -0.5 * D_IDX ** -0.5),
        gates=jax.nn.sigmoid(g.reshape(n, t, 3, D_MODEL)),
    )


def diff_lambda(lam_vecs, layer):
    lam_init = 0.8 - 0.6 * math.exp(-0.3 * layer)
    lv = lam_vecs.astype(jnp.float32)
    lam = jnp.exp(jnp.sum(lv[0] * lv[1])) - jnp.exp(jnp.sum(lv[2] * lv[3])) + lam_init
    return lam, lam_init


def attend(qa, qb, qc, iq, iw, cq_b, q_pos, ka, va, kb, vb, kc, vc, ik, ck_b, k_pos, lam, lam_init, a_g, topk):
    f32 = jnp.float32
    n, nq = qa.shape[:2]
    scale = DH ** -0.5
    mask = (k_pos[None, :] <= q_pos[:, None]) & (k_pos[None, :] >= 0)
    s = jnp.einsum('nqhmd,nkhmd->nmhqk', qa, ka).astype(f32) * scale
    p = jax.nn.softmax(jnp.where(mask, s, NEG_INF), axis=-1)
    oa = jnp.einsum('nhqk,nkhe->nqhe', (p[:, 0] - lam * p[:, 1]).astype(va.dtype), va).astype(f32)
    oa = oa * lax.rsqrt(jnp.mean(jnp.square(oa), -1, keepdims=True) + RMS_EPS) * a_g * (1.0 - lam_init)
    s = jnp.einsum('nqhd,nkhd->nhqk', qb, kb).astype(f32) * scale
    s = s + jnp.swapaxes(cq_b, 1, 2)[:, :, :, None] - jnp.swapaxes(ck_b, 1, 2)[:, :, None, :]
    p = jax.nn.softmax(jnp.where(mask, s, NEG_INF), axis=-1)
    ob = jnp.einsum('nhqk,nkhd->nqhd', p.astype(vb.dtype), vb)
    isc = jax.nn.relu(jnp.einsum('nqhd,nkd->nqhk', iq, ik).astype(f32))
    isc = jnp.einsum('nqhk,nqh->nqk', isc, iw.astype(f32))
    _, idx = lax.top_k(jnp.where(mask, isc, NEG_INF), topk)
    sel_pos = k_pos[idx]
    sel_ok = (sel_pos <= q_pos[None, :, None]) & (sel_pos >= 0)
    take = jax.vmap(lambda a, i: a[i])
    kg = take(kc, idx)
    vg = take(vc, idx)
    s = jnp.einsum('nqhd,nqkhd->nhqk', qc, kg).astype(f32) * scale
    p = jax.nn.softmax(jnp.where(sel_ok[:, None], s, NEG_INF), axis=-1)
    oc = jnp.einsum('nhqk,nqkhd->nqhd', p.astype(vc.dtype), vg)
    return (oa.astype(va.dtype).reshape(n, nq, W_A), ob.reshape(n, nq, W_B), oc.reshape(n, nq, W_C))


def moe(x, w_router, b_router, w1, b1, w2, b2):
    f32 = jnp.float32
    logits = (jnp.einsum('ntd,de->nte', x, w_router) + b_router).astype(f32)
    top_v, top_i = lax.top_k(logits, TOP_K_EXPERTS)
    wts = jax.nn.softmax(top_v, axis=-1)
    gate = jnp.sum(jax.nn.one_hot(top_i, N_EXPERTS, dtype=f32) * wts[..., None], axis=-2)
    out = jnp.zeros(x.shape, f32)
    for e in range(N_EXPERTS):
        h = jnp.einsum('ntd,df->ntf', x, w1[e]) + b1[e]
        glu = jnp.minimum(h[..., ::2], SWIGLU_LIMIT)
        lin = jnp.clip(h[..., 1::2], -SWIGLU_LIMIT, SWIGLU_LIMIT)
        a = glu * jax.nn.sigmoid(SWIGLU_ALPHA * glu) * (lin + 1.0)
        y = jnp.einsum('ntf,fd->ntd', a, w2[e]) + b2[e]
        out = out + gate[..., e:e + 1] * y
    return out.astype(x.dtype)


def merge_and_channel(x, oa, ob, oc, gates, lw):
    (_, _, _, _, w_br_a, w_br_b, w_br_c, w_out, ln1_g, ln1_b, ln2_g, ln2_b,
     w_router, b_router, w_mlp1, b_mlp1, w_mlp2, b_mlp2) = lw
    merged = (gates[:, :, 0] * jnp.einsum('ntc,cd->ntd', oa, w_br_a)
              + gates[:, :, 1] * jnp.einsum('ntc,cd->ntd', ob, w_br_b)
              + gates[:, :, 2] * jnp.einsum('ntc,cd->ntd', oc, w_br_c))
    mix = jnp.einsum('ntd,de->nte', merged, w_out)
    x = layer_norm(DEEPNORM_ALPHA * x + mix, ln1_g, ln1_b)
    x = layer_norm(DEEPNORM_ALPHA * x + moe(x, w_router, b_router, w_mlp1, b_mlp1, w_mlp2, b_mlp2), ln2_g, ln2_b)
    return x


def to_blocks(a):
    n, t = a.shape[:2]
    return jnp.moveaxis(a.reshape((n, t // BLOCK_Q, BLOCK_Q) + a.shape[2:]), 1, 0)


def from_blocks(a):
    nb, n, bq = a.shape[:3]
    return jnp.moveaxis(a, 0, 1).reshape((n, nb * bq) + a.shape[3:])


def prompt_layer(x, pos, lw, layer, topk):
    n, t, _ = x.shape
    pr = project(x, pos, lw[0], lw[1])
    lam, lam_init = diff_lambda(lw[2], layer)
    logf = jnp.where((pos >= 0)[None, :, None], pr['logf'], 0.0)
    csum = jnp.cumsum(logf, axis=1)
    keys = (pr['ak'], pr['av'], pr['bk'], pr['bv'], pr['ck'], pr['cv'], pr['ik'])

    def one_block(blk):
        qa, qb, qc, iq, iw, cq_b, q_pos = blk
        return attend(qa, qb, qc, iq, iw, cq_b, q_pos, *keys, csum, pos, lam, lam_init, lw[3], topk)

    blocks = (to_blocks(pr['aq']), to_blocks(pr['bq']), to_blocks(pr['cq']), to_blocks(pr['iq']),
              to_blocks(pr['iw']), to_blocks(csum), pos.reshape(t // BLOCK_Q, BLOCK_Q))
    oa, ob, oc = lax.map(one_block, blocks)
    x = merge_and_channel(x, from_blocks(oa), from_blocks(ob), from_blocks(oc), pr['gates'], lw)
    rows = tuple(a[:, META_PAD:] for a in (pr['ak'].reshape(n, t, H_A, 2 * DH), pr['av'], pr['bk'], pr['bv'],
                                            logf, pr['ck'], pr['cv'], pr['ik']))
    return x, rows


def sample_layer(x, q_pos, k_pos, caches, page_table, lw, layer, topk):
    n, t, _ = x.shape
    pr = project(x, q_pos, lw[0], lw[1])
    lam, lam_init = diff_lambda(lw[2], layer)
    past = []
    for c in caches:
        g = c[layer, page_table]
        past.append(g.reshape((n, g.shape[1] * g.shape[2]) + g.shape[3:]))
    new = (pr['ak'].reshape(n, t, H_A, 2 * DH), pr['av'], pr['bk'], pr['bv'], pr['logf'],
           pr['ck'], pr['cv'], pr['ik'])
    ka, va, kb, vb, lf, kc, vc, ik = [jnp.concatenate([pa.astype(r.dtype), r], axis=1) for pa, r in zip(past, new)]
    ka = ka.reshape(n, -1, H_A, 2, DH)
    csum = jnp.cumsum(lf, axis=1)
    oa, ob, oc = attend(pr['aq'], pr['bq'], pr['cq'], pr['iq'], pr['iw'], csum[:, -t:], q_pos,
                        ka, va, kb, vb, kc, vc, ik, csum, k_pos, lam, lam_init, lw[3], topk)
    x = merge_and_channel(x, oa, ob, oc, pr['gates'], lw)
    return x, new


def setup_inputs(seed: int = 0) -> dict:
    key = jax.random.key(seed)
    ks = list(jax.random.split(key, 40))
    f32 = jnp.float32

    def normal(shape, scale=1.0):
        return jax.random.normal(ks.pop(), shape, f32) * scale

    n_pages = PAST_LEN // PAGE_SIZE
    n_used = DEC_BATCH * n_pages
    n_pool = n_used + max(1, n_used // 4)
    inp = {}
    inp['x_prompt'] = normal((BATCH, SEQ, D_MODEL))
    inp['x_sample'] = normal((DEC_BATCH, DEC_SEQ, D_MODEL))
    inp['cache_a_k'] = normal((DEPTH, n_pool, PAGE_SIZE, H_A, 2 * DH))
    inp['cache_a_v'] = normal((DEPTH, n_pool, PAGE_SIZE, H_A, 2 * DH))
    inp['cache_b_k'] = normal((DEPTH, n_pool, PAGE_SIZE, H_B, DH))
    inp['cache_b_v'] = normal((DEPTH, n_pool, PAGE_SIZE, H_B, DH))
    inp['cache_b_logf'] = jax.nn.log_sigmoid(FORGET_BIAS + normal((DEPTH, n_pool, PAGE_SIZE, H_B)))
    inp['cache_c_k'] = normal((DEPTH, n_pool, PAGE_SIZE, H_C, DH))
    inp['cache_c_v'] = normal((DEPTH, n_pool, PAGE_SIZE, H_C, DH))
    inp['cache_c_idx_k'] = normal((DEPTH, n_pool, PAGE_SIZE, D_IDX))
    inp['page_table'] = jax.random.permutation(ks.pop(), n_pool)[:n_used].reshape(DEC_BATCH, n_pages).astype(jnp.int32)
    inp['meta'] = normal((N_META, D_MODEL))
    inp['w_in'] = normal((DEPTH, D_MODEL, N_IN), D_MODEL ** -0.5)
    f_off = sum(PROJ_SIZES[:6])
    inp['b_in'] = normal((DEPTH, N_IN), 0.02).at[:, f_off:f_off + H_B].add(FORGET_BIAS)
    inp['a_lambda'] = normal((DEPTH, 4, DH), 0.1)
    inp['a_norm_g'] = 1.0 + normal((DEPTH, 2 * DH), 0.02)
    inp['w_br_a'] = normal((DEPTH, W_A, D_MODEL), W_A ** -0.5)
    inp['w_br_b'] = normal((DEPTH, W_B, D_MODEL), W_B ** -0.5)
    inp['w_br_c'] = normal((DEPTH, W_C, D_MODEL), W_C ** -0.5)
    inp['w_out'] = normal((DEPTH, D_MODEL, D_MODEL), D_MODEL ** -0.5 * DEEPNORM_BETA)
    inp['ln1_g'] = 1.0 + normal((DEPTH, D_MODEL), 0.02)
    inp['ln1_b'] = normal((DEPTH, D_MODEL), 0.02)
    inp['ln2_g'] = 1.0 + normal((DEPTH, D_MODEL), 0.02)
    inp['ln2_b'] = normal((DEPTH, D_MODEL), 0.02)
    inp['w_router'] = normal((DEPTH, D_MODEL, N_EXPERTS), D_MODEL ** -0.5)
    inp['b_router'] = normal((DEPTH, N_EXPERTS), 0.01)
    inp['w_mlp1'] = normal((DEPTH, N_EXPERTS, D_MODEL, 2 * D_FF), D_MODEL ** -0.5)
    inp['b_mlp1'] = normal((DEPTH, N_EXPERTS, 2 * D_FF), 0.02)
    inp['w_mlp2'] = normal((DEPTH, N_EXPERTS, D_FF, D_MODEL), D_FF ** -0.5 * DEEPNORM_BETA)
    inp['b_mlp2'] = normal((DEPTH, N_EXPERTS, D_MODEL), 0.02)
    return inp


def reference(x_prompt, x_sample, cache_a_k, cache_a_v, cache_b_k, cache_b_v, cache_b_logf, cache_c_k,
              cache_c_v, cache_c_idx_k, page_table, meta, w_in, b_in, a_lambda, a_norm_g, w_br_a, w_br_b,
              w_br_c, w_out, ln1_g, ln1_b, ln2_g, ln2_b, w_router, b_router, w_mlp1, b_mlp1, w_mlp2, b_mlp2):
    weights = (w_in, b_in, a_lambda, a_norm_g, w_br_a, w_br_b, w_br_c, w_out, ln1_g, ln1_b, ln2_g, ln2_b,
               w_router, b_router, w_mlp1, b_mlp1, w_mlp2, b_mlp2)
    caches = (cache_a_k, cache_a_v, cache_b_k, cache_b_v, cache_b_logf, cache_c_k, cache_c_v, cache_c_idx_k)
    n_p, seq, _ = x_prompt.shape
    dec = x_sample.shape[1]
    past_len = page_table.shape[1] * PAGE_SIZE
    topk_prompt = min(TOPK_MAX, seq // 4)
    topk_sample = min(TOPK_MAX, (past_len + dec) // 4)

    xp = jnp.concatenate([jnp.zeros((n_p, META_PAD, D_MODEL), x_prompt.dtype),
                          jnp.broadcast_to(meta.astype(x_prompt.dtype)[None], (n_p, N_META, D_MODEL)),
                          x_prompt], axis=1)
    pos_p = jnp.arange(seq + BLOCK_Q, dtype=jnp.int32) - META_PAD
    xs = x_sample
    q_pos_s = past_len + jnp.arange(dec, dtype=jnp.int32)
    k_pos_s = jnp.arange(past_len + dec, dtype=jnp.int32)

    rows_p, rows_s = [], []
    for layer in range(DEPTH):
        lw = tuple(w[layer] for w in weights)
        xp, r = prompt_layer(xp, pos_p, lw, layer, topk_prompt)
        rows_p.append(r)
        xs, r = sample_layer(xs, q_pos_s, k_pos_s, caches, page_table, lw, layer, topk_sample)
        rows_s.append(r)

    st = lambda rows, i: jnp.stack([r[i] for r in rows])
    return (xp[:, BLOCK_Q:], xs,
            st(rows_p, 0), st(rows_p, 1), st(rows_p, 2), st(rows_p, 3),
            st(rows_p, 4), st(rows_p, 5), st(rows_p, 6), st(rows_p, 7),
            st(rows_s, 0), st(rows_s, 1), st(rows_s, 2), st(rows_s, 3),
            st(rows_s, 4), st(rows_s, 5), st(rows_s, 6), st(rows_s, 7))
```

```python
import functools
import math

import numpy as np
import jax
import jax.numpy as jnp
from jax import lax
from jax.experimental import pallas as pl
from jax.experimental.pallas import tpu as pltpu

F32 = jnp.float32
BF16 = jnp.bfloat16
I32 = jnp.int32

D_MODEL = 1024
DH = 64
H_A = 4
H_B = 8
H_C = 8
H_I = 4
D_IDX = 64
TOPK_MAX = 256
N_META = 16
BLOCK_Q = 128
META_PAD = BLOCK_Q - N_META
ROT_DIM = DH // 4
ROPE_THETA = 500000.0
N_EXPERTS = 32
TOP_K_EXPERTS = 4
SWIGLU_ALPHA = 1.702
SWIGLU_LIMIT = 7.0
LN_EPS = 1e-5
RMS_EPS = 1e-5
NEG = -1e30
M_INIT = float(np.finfo(np.float32).min)
PAGE = 128
LANES = 128
W_A = H_A * 2 * DH
W_B = H_B * DH
W_C = H_C * DH
PROJ_SIZES = (W_A, W_A, W_A, W_B, W_B, W_B, H_B, W_C, W_C, W_C, H_I * D_IDX, D_IDX, H_I, 3 * D_MODEL)
SCALE = DH ** -0.5
IW_SCALE = H_I ** -0.5 * D_IDX ** -0.5

AQ, AK, AV, BQ, BK, BV, CQ, CK, CV, IQ, IK, MISC = 0, 4, 8, 12, 16, 20, 24, 28, 32, 36, 38, 39
N_MAIN = 40 * LANES
MISC_BF = 16
MISC_IW = 24
PROJ_TN = 512
ROPE_TILES = (1, 1, 0, 0, 0, 0, 1, 1, 0, 1)
VMEM_LIMIT = 56 * 1024 * 1024
INT_MIN = -2 ** 31

NT = (((1,), (1,)), ((), ()))


def _pick(n, cands):
    for c in cands:
        if n % c == 0:
            return c
    raise ValueError(f"no tile for {n}")


def _cparams(sem):
    return pltpu.CompilerParams(dimension_semantics=sem, vmem_limit_bytes=VMEM_LIMIT)


def _log_sigmoid(x):
    return jnp.minimum(x, 0.0) - jnp.log1p(jnp.exp(-jnp.abs(x)))


def _lane_scan(x):
    lane = lax.broadcasted_iota(I32, x.shape, 1)
    s = 1
    while s < LANES:
        x = x + jnp.where(lane >= s, pltpu.roll(x, s, 1), 0.0)
        s *= 2
    return x


def _rope_tile(x, cos, sa, sb):
    outs = []
    for c in range(x.shape[1] // LANES):
        xc = x[:, c * LANES:(c + 1) * LANES]
        up = pltpu.roll(xc, LANES - ROT_DIM // 2, 1)
        dn = pltpu.roll(xc, ROT_DIM // 2, 1)
        outs.append(xc * cos + up * sa + dn * sb)
    return jnp.concatenate(outs, axis=1)


def _proj_kernel(flags_ref, x_ref, w_ref, b_ref, cos_ref, sa_ref, sb_ref, o32_ref, o16_ref):
    j = pl.program_id(1)
    acc = jnp.dot(x_ref[...], w_ref[...], preferred_element_type=F32) + b_ref[...]

    @pl.when(flags_ref[j] == 0)
    def _():
        o32_ref[...] = acc
        o16_ref[...] = acc.astype(BF16)

    @pl.when(flags_ref[j] == 1)
    def _():
        r = _rope_tile(acc, cos_ref[...], sa_ref[...], sb_ref[...])
        o32_ref[...] = r
        o16_ref[...] = r.astype(BF16)


def _project_main(x16, w, b, cos, sa, sb, tm):
    m = x16.shape[0]
    n = w.shape[1]
    flags = jnp.asarray(ROPE_TILES, I32)
    grid_spec = pltpu.PrefetchScalarGridSpec(
        num_scalar_prefetch=1,
        grid=(m // tm, n // PROJ_TN),
        in_specs=[
            pl.BlockSpec((tm, D_MODEL), lambda i, j, f: (i, 0)),
            pl.BlockSpec((D_MODEL, PROJ_TN), lambda i, j, f: (0, j)),
            pl.BlockSpec((1, PROJ_TN), lambda i, j, f: (0, j)),
            pl.BlockSpec((tm, LANES), lambda i, j, f: (i, 0)),
            pl.BlockSpec((tm, LANES), lambda i, j, f: (i, 0)),
            pl.BlockSpec((tm, LANES), lambda i, j, f: (i, 0)),
        ],
        out_specs=[
            pl.BlockSpec((tm, PROJ_TN), lambda i, j, f: (i, j)),
            pl.BlockSpec((tm, PROJ_TN), lambda i, j, f: (i, j)),
        ],
    )
    return pl.pallas_call(
        _proj_kernel,
        grid_spec=grid_spec,
        out_shape=[jax.ShapeDtypeStruct((m, n), F32), jax.ShapeDtypeStruct((m, n), BF16)],
        compiler_params=_cparams(("parallel", "arbitrary")),
        name="proj_main",
    )(flags, x16, w, b, cos, sa, sb)


def _gate_kernel(x_ref, w_ref, b_ref, o_ref):
    acc = jnp.dot(x_ref[...], w_ref[...], preferred_element_type=F32) + b_ref[...]
    o_ref[...] = jax.nn.sigmoid(acc)


def _project_gates(x16, w, b, tm):
    m = x16.shape[0]
    n = w.shape[1]
    return pl.pallas_call(
        _gate_kernel,
        grid=(m // tm, n // PROJ_TN),
        in_specs=[
            pl.BlockSpec((tm, D_MODEL), lambda i, j: (i, 0)),
            pl.BlockSpec((D_MODEL, PROJ_TN), lambda i, j: (0, j)),
            pl.BlockSpec((1, PROJ_TN), lambda i, j: (0, j)),
        ],
        out_specs=pl.BlockSpec((tm, PROJ_TN), lambda i, j: (i, j)),
        out_shape=jax.ShapeDtypeStruct((m, n), F32),
        compiler_params=_cparams(("parallel", "arbitrary")),
        name="proj_gates",
    )(x16, w, b)


def _cumsum_kernel(bf_ref, logf_ref, csum_ref, *, t):
    def body(c, carry):
        sl = pl.ds(pl.multiple_of(c * LANES, LANES), LANES)
        tok = c * LANES + lax.broadcasted_iota(I32, (H_B, LANES), 1)
        lf = jnp.where(tok >= META_PAD, _log_sigmoid(bf_ref[0, :, sl]), 0.0)
        logf_ref[0, :, sl] = lf
        cs = _lane_scan(lf) + carry
        csum_ref[0, :, sl] = cs
        return cs[:, LANES - 1:LANES]

    lax.fori_loop(0, t // LANES, body, jnp.zeros((H_B, 1), F32))


def _prompt_cumsum(bf_t):
    n, h, t = bf_t.shape
    return pl.pallas_call(
        functools.partial(_cumsum_kernel, t=t),
        grid=(n,),
        in_specs=[pl.BlockSpec((1, h, t), lambda b: (b, 0, 0))],
        out_specs=[pl.BlockSpec((1, h, t), lambda b: (b, 0, 0)),
                   pl.BlockSpec((1, h, t), lambda b: (b, 0, 0))],
        out_shape=[jax.ShapeDtypeStruct((n, h, t), F32), jax.ShapeDtypeStruct((n, h, t), F32)],
        compiler_params=_cparams(("parallel",)),
        name="prompt_cumsum",
    )(bf_t)


def _topk_select(sc_ref, key_ref, out_ref, nchunks, k, rows, out_dtype):
    def sl_of(c):
        return pl.ds(pl.multiple_of(c * LANES, LANES), LANES)

    def make_keys(c, _):
        bits = pltpu.bitcast(sc_ref[:, sl_of(c)], I32)
        key_ref[:, sl_of(c)] = jnp.where(bits < 0, bits ^ jnp.int32(0x7FFFFFFF), bits)
        return 0

    lax.fori_loop(0, nchunks, make_keys, 0)

    def count(pred):
        def body(c, acc):
            return acc + jnp.where(pred(key_ref[:, sl_of(c)]), 1, 0).astype(I32)
        acc = lax.fori_loop(0, nchunks, body, jnp.zeros((rows, LANES), I32))
        return jnp.sum(acc, axis=1, keepdims=True)

    c0 = count(lambda key: key >= 0)
    t0 = jnp.where(c0 >= k, jnp.int32(0), jnp.int32(INT_MIN))

    def bit_body(i, t):
        cand = t | jnp.left_shift(jnp.int32(1), 30 - i)
        cnt = count(lambda key: key >= cand)
        return jnp.where(cnt >= k, cand, t)

    thr = lax.fori_loop(0, 31, bit_body, t0)
    need = (k - count(lambda key: key > thr)).astype(F32)
    upper = (lax.broadcasted_iota(I32, (LANES, LANES), 0)
             < lax.broadcasted_iota(I32, (LANES, LANES), 1)).astype(BF16)

    def sel_body(c, carry):
        key = key_ref[:, sl_of(c)]
        eq = key == thr
        eqf = jnp.where(eq, 1.0, 0.0)
        rank = jnp.dot(eqf.astype(BF16), upper, preferred_element_type=F32) + carry
        sel = (key > thr) | (eq & (rank < need))
        valid = sc_ref[:, sl_of(c)] > 0.5 * NEG
        out_ref[:, sl_of(c)] = jnp.where(sel & valid, 0.0, NEG).astype(out_dtype)
        return carry + jnp.sum(eqf, axis=1, keepdims=True)

    lax.fori_loop(0, nchunks, sel_body, jnp.zeros((rows, 1), F32))


def _index_scores(iq_heads, iw, kmat):
    acc = None
    for h in range(H_I):
        s = lax.dot_general(iq_heads[h], kmat, NT, preferred_element_type=F32)
        term = jnp.maximum(s, 0.0) * iw[:, h:h + 1]
        acc = term if acc is None else acc + term
    return acc


def _prompt_index_kernel(iq_ref, ik_ref, misc_ref, bias_ref, sc_scr, key_scr, *, k):
    i = pl.program_id(1)
    nchunks = i + 1
    lo = lax.broadcasted_iota(I32, (1, LANES), 1) < DH
    zero = jnp.zeros((BLOCK_Q, LANES), BF16)
    iq_heads = []
    for pair in range(H_I // 2):
        blk = iq_ref[:, pair * LANES:(pair + 1) * LANES]
        iq_heads.append(jnp.where(lo, blk, zero))
        iq_heads.append(jnp.where(lo, zero, blk))
    iw = misc_ref[...][:, MISC_IW:MISC_IW + H_I] * IW_SCALE
    qidx = i * BLOCK_Q + lax.broadcasted_iota(I32, (BLOCK_Q, 1), 0)

    def chunk(c, _):
        sl = pl.ds(pl.multiple_of(c * LANES, LANES), LANES)
        acc = _index_scores(iq_heads, iw, ik_ref[sl, :])
        kidx = c * LANES + lax.broadcasted_iota(I32, (1, LANES), 1)
        ok = (kidx <= qidx) & (kidx >= META_PAD)
        sc_scr[:, sl] = jnp.where(ok, acc, NEG)
        return 0

    lax.fori_loop(0, nchunks, chunk, 0)
    bias_ref[...] = jnp.full(bias_ref.shape, NEG, bias_ref.dtype)
    _topk_select(sc_scr, key_scr, bias_ref.at[0], nchunks, k, BLOCK_Q, bias_ref.dtype)


def _prompt_index(p16, p32, n, t, k):
    nq = t // BLOCK_Q
    return pl.pallas_call(
        functools.partial(_prompt_index_kernel, k=k),
        grid=(n, nq),
        in_specs=[
            pl.BlockSpec((BLOCK_Q, 2 * LANES), lambda b, i: (b * nq + i, IQ // 2)),
            pl.BlockSpec((t, LANES), lambda b, i: (b, IK)),
            pl.BlockSpec((BLOCK_Q, LANES), lambda b, i: (b * nq + i, MISC)),
        ],
        out_specs=pl.BlockSpec((1, BLOCK_Q, t), lambda b, i: (b, i, 0)),
        out_shape=jax.ShapeDtypeStruct((n, t, t), BF16),
        scratch_shapes=[pltpu.VMEM((BLOCK_Q, t), F32), pltpu.VMEM((BLOCK_Q, t), I32)],
        compiler_params=_cparams(("parallel", "arbitrary")),
        name="prompt_index_topk",
    )(p16, p16, p32)


def _diff_lambda(lam_ref, layer):
    lam_init = 0.8 - 0.6 * math.exp(-0.3 * layer)
    lv = lam_ref[...]
    a = jnp.sum(lv[0:1] * lv[1:2], axis=1, keepdims=True)
    b = jnp.sum(lv[2:3] * lv[3:4], axis=1, keepdims=True)
    return jnp.exp(a) - jnp.exp(b) + lam_init, lam_init


def _diff_finish(o0, o1, lam, lam_init, g):
    o = o0 - lam * o1
    o = o * lax.rsqrt(jnp.mean(jnp.square(o), axis=-1, keepdims=True) + RMS_EPS)
    return o * g * (1.0 - lam_init)


def _flash_kernel(qi_ref, kj_ref, *refs, mode, tq, tk, layer):
    if mode == "A":
        q_ref, k_ref, v_ref, lam_ref, g_ref, o_ref, m_scr, l_scr, acc_scr = refs
    elif mode == "B":
        q_ref, k_ref, v_ref, cq_ref, ck_ref, o_ref, m_scr, l_scr, acc_scr, cq_scr = refs
    else:
        q_ref, k_ref, v_ref, bias_ref, o_ref, m_scr, l_scr, acc_scr = refs
    cb = pl.program_id(1)
    step = pl.program_id(2)
    i = qi_ref[step]
    j = kj_ref[step]
    lo = lax.broadcasted_iota(I32, (1, LANES), 1) < DH

    @pl.when(j == 0)
    def _():
        m_scr[...] = jnp.full(m_scr.shape, M_INIT, F32)
        l_scr[...] = jnp.zeros(l_scr.shape, F32)
        acc_scr[...] = jnp.zeros(acc_scr.shape, F32)
        if mode == "B":
            lane8 = lax.broadcasted_iota(I32, (1, H_B), 1)
            for u in range(2):
                cq_scr[u] = jnp.sum(jnp.where(lane8 == 2 * cb + u, cq_ref[0], 0.0), axis=1, keepdims=True)

    def update(masked):
        q = q_ref[...] * jnp.asarray(SCALE, BF16)
        zero = jnp.zeros_like(q)
        k = k_ref[...]
        v = v_ref[...]
        if masked:
            qidx = i * tq + lax.broadcasted_iota(I32, (tq, 1), 0)
            kidx = j * tk + lax.broadcasted_iota(I32, (1, tk), 1)
            ok = (kidx <= qidx) & (kidx >= META_PAD)
        for u in range(2):
            qu = jnp.where(lo, q, zero) if u == 0 else jnp.where(lo, zero, q)
            s = lax.dot_general(qu, k, NT, preferred_element_type=F32)
            if mode == "B":
                s = s + cq_scr[u] - ck_ref[0, pl.ds(2 * cb + u, 1), :]
            if mode == "C":
                s = s + bias_ref[0].astype(F32)
            if masked:
                s = jnp.where(ok, s, NEG)
            m_prev = m_scr[u]
            m_new = jnp.maximum(m_prev, jnp.max(s, axis=1, keepdims=True))
            alpha = jnp.exp(m_prev - m_new)
            p = jnp.exp(s - m_new)
            l_scr[u] = alpha * l_scr[u] + jnp.sum(p, axis=1, keepdims=True)
            acc_scr[u] = alpha * acc_scr[u] + jnp.dot(p.astype(BF16), v, preferred_element_type=F32)
            m_scr[u] = m_new

    if mode == "C":
        update(False)
    else:
        edge = (j == i) | (j == 0)

        @pl.when(edge)
        def _():
            update(True)

        @pl.when(jnp.logical_not(edge))
        def _():
            update(False)

    @pl.when(j == i)
    def _():
        o0 = acc_scr[0] / l_scr[0]
        o1 = acc_scr[1] / l_scr[1]
        if mode == "A":
            lam, lam_init = _diff_lambda(lam_ref, layer)
            o_ref[...] = _diff_finish(o0, o1, lam, lam_init, g_ref[...]).astype(o_ref.dtype)
        else:
            o_ref[...] = jnp.where(lo, o0, o1).astype(o_ref.dtype)


def _prompt_flash(mode, p16, extras, n, t, layer):
    tq = _pick(t, (640, 512, 384, 256, 128))
    tk = tq
    nq = t // tq
    qi = np.array([i for i in range(nq) for _ in range(i + 1)], np.int32)
    kj = np.array([j for i in range(nq) for j in range(i + 1)], np.int32)
    qcol, kcol, vcol = {"A": (AQ, AK, AV), "B": (BQ, BK, BV), "C": (CQ, CK, CV)}[mode]
    in_specs = [
        pl.BlockSpec((tq, LANES), lambda b, c, s, qi, kj: (b * nq + qi[s], qcol + c)),
        pl.BlockSpec((tk, LANES), lambda b, c, s, qi, kj: (b * nq + kj[s], kcol + c)),
        pl.BlockSpec((tk, LANES), lambda b, c, s, qi, kj: (b * nq + kj[s], vcol + c)),
    ]
    scratch = [pltpu.VMEM((2, tq, 1), F32), pltpu.VMEM((2, tq, 1), F32), pltpu.VMEM((2, tq, LANES), F32)]
    if mode == "A":
        in_specs += [pl.BlockSpec((4, DH), lambda b, c, s, qi, kj: (0, 0)),
                     pl.BlockSpec((1, 2 * DH), lambda b, c, s, qi, kj: (0, 0))]
    elif mode == "B":
        in_specs += [pl.BlockSpec((1, tq, H_B), lambda b, c, s, qi, kj: (b, qi[s], 0)),
                     pl.BlockSpec((1, H_B, tk), lambda b, c, s, qi, kj: (b, 0, kj[s]))]
        scratch += [pltpu.VMEM((2, tq, 1), F32)]
    else:
        in_specs += [pl.BlockSpec((1, tq, tk), lambda b, c, s, qi, kj: (b, qi[s], kj[s]))]
    grid_spec = pltpu.PrefetchScalarGridSpec(
        num_scalar_prefetch=2,
        grid=(n, 4, len(qi)),
        in_specs=in_specs,
        out_specs=pl.BlockSpec((tq, LANES), lambda b, c, s, qi, kj: (b * nq + qi[s], c)),
        scratch_shapes=scratch,
    )
    return pl.pallas_call(
        functools.partial(_flash_kernel, mode=mode, tq=tq, tk=tk, layer=layer),
        grid_spec=grid_spec,
        out_shape=jax.ShapeDtypeStruct((n * t, 4 * LANES), BF16),
        compiler_params=_cparams(("parallel", "parallel", "arbitrary")),
        name=f"prompt_flash_{mode}",
    )(jnp.asarray(qi), jnp.asarray(kj), p16, p16, p16, *extras)


def _pad_rows(x, rows):
    return jnp.concatenate([x, jnp.zeros((rows - x.shape[0], x.shape[1]), x.dtype)], axis=0)


def _cols(ref, start, width):
    blk = start // LANES
    off = start - blk * LANES
    x = ref[:, blk * LANES:(blk + 1) * LANES]
    return x if width == LANES else x[:, off:off + width]


def _sample_pre_kernel(pt_ref, proj_ref, idxk_ref, logf_ref, bias_ref, csum_ref, cq_ref, lnew_ref,
                       sc_scr, key_scr, carry_scr, *, pages, k, dec):
    p = pl.program_id(1)
    iq_heads = [_cols(proj_ref, IQ * LANES + h * D_IDX, D_IDX).astype(BF16) for h in range(H_I)]
    iw = _cols(proj_ref, MISC * LANES + MISC_IW, H_I) * IW_SCALE
    eye = (lax.broadcasted_iota(I32, (H_B, H_B), 0) == lax.broadcasted_iota(I32, (H_B, H_B), 1)).astype(F32)
    cols = pl.ds(pl.multiple_of(p * LANES, LANES), LANES)

    @pl.when(p == 0)
    def _():
        carry_scr[...] = jnp.zeros(carry_scr.shape, F32)

    def put_csum(logf_rows):
        lt = lax.dot_general(eye, logf_rows, NT, precision=lax.Precision.HIGHEST, preferred_element_type=F32)
        cs = _lane_scan(lt) + carry_scr[...]
        csum_ref[0, :, cols] = cs
        carry_scr[...] = cs[:, LANES - 1:LANES]
        return cs

    @pl.when(p < pages)
    def _():
        sc_scr[:, cols] = _index_scores(iq_heads, iw, idxk_ref[0, 0].astype(BF16))
        put_csum(logf_ref[0, 0])

    @pl.when(p == pages)
    def _():
        knew = _pad_rows(_cols(proj_ref, IK * LANES, D_IDX).astype(BF16), LANES)
        s = _index_scores(iq_heads, iw, knew)
        row = lax.broadcasted_iota(I32, (dec, LANES), 0)
        lane = lax.broadcasted_iota(I32, (dec, LANES), 1)
        sc_scr[:, cols] = jnp.where(lane <= row, s, NEG)
        lnew = _log_sigmoid(_cols(proj_ref, MISC * LANES + MISC_BF, H_B))
        lnew_ref[0] = lnew
        cs = put_csum(_pad_rows(lnew, LANES))
        pick = (lax.broadcasted_iota(I32, (dec, LANES), 0) == lax.broadcasted_iota(I32, (dec, LANES), 1)).astype(F32)
        cq_ref[0] = lax.dot_general(pick, cs, NT, precision=lax.Precision.HIGHEST, preferred_element_type=F32)
        _topk_select(sc_scr, key_scr, bias_ref.at[0], pages + 1, k, dec, F32)


def _sample_pre(pt_flat, p32s, cache_idx_k, cache_logf, layer, nseq, dec, pages, k):
    width = (pages + 1) * LANES

    def page_map(b, p, pt):
        return (layer, pt[b * pages + jnp.minimum(p, pages - 1)], 0, 0)

    grid_spec = pltpu.PrefetchScalarGridSpec(
        num_scalar_prefetch=1,
        grid=(nseq, pages + 1),
        in_specs=[
            pl.BlockSpec((dec, N_MAIN), lambda b, p, pt: (b, 0)),
            pl.BlockSpec((1, 1, PAGE, D_IDX), page_map),
            pl.BlockSpec((1, 1, PAGE, H_B), page_map),
        ],
        out_specs=[
            pl.BlockSpec((1, dec, width), lambda b, p, pt: (b, 0, 0)),
            pl.BlockSpec((1, H_B, width), lambda b, p, pt: (b, 0, 0)),
            pl.BlockSpec((1, dec, H_B), lambda b, p, pt: (b, 0, 0)),
            pl.BlockSpec((1, dec, H_B), lambda b, p, pt: (b, 0, 0)),
        ],
        scratch_shapes=[pltpu.VMEM((dec, width), F32), pltpu.VMEM((dec, width), I32), pltpu.VMEM((H_B, 1), F32)],
    )
    return pl.pallas_call(
        functools.partial(_sample_pre_kernel, pages=pages, k=k, dec=dec),
        grid_spec=grid_spec,
        out_shape=[jax.ShapeDtypeStruct((nseq, dec, width), F32),
                   jax.ShapeDtypeStruct((nseq, H_B, width), F32),
                   jax.ShapeDtypeStruct((nseq, dec, H_B), F32),
                   jax.ShapeDtypeStruct((nseq, dec, H_B), F32)],
        compiler_params=_cparams(("parallel", "arbitrary")),
        name="sample_index_topk",
    )(pt_flat, p32s, cache_idx_k, cache_logf)


def _sample_attn_kernel(pt_ref, proj_ref, ka_ref, va_ref, kb_ref, vb_ref, kc_ref, vc_ref,
                        csum_ref, cq_ref, bias_ref, lam_ref, g_ref,
                        oa_ref, ob_ref, oc_ref,
                        ma, la, acca, mb, lb, accb, mc, lc, accc, *, pages, layer, dec):
    p = pl.program_id(1)
    lo = lax.broadcasted_iota(I32, (1, LANES), 1) < DH
    cols = pl.ds(pl.multiple_of(p * LANES, LANES), LANES)

    @pl.when(p == 0)
    def _():
        for m_ref, l_ref, acc_ref in ((ma, la, acca), (mb, lb, accb), (mc, lc, accc)):
            m_ref[...] = jnp.full(m_ref.shape, M_INIT, F32)
            l_ref[...] = jnp.zeros(l_ref.shape, F32)
            acc_ref[...] = jnp.zeros(acc_ref.shape, F32)

    def update(m_ref, l_ref, acc_ref, idx, s, v):
        m_prev = m_ref[idx]
        m_new = jnp.maximum(m_prev, jnp.max(s, axis=1, keepdims=True))
        alpha = jnp.exp(m_prev - m_new)
        pr = jnp.exp(s - m_new)
        l_ref[idx] = alpha * l_ref[idx] + jnp.sum(pr, axis=1, keepdims=True)
        acc_ref[idx] = alpha * acc_ref[idx] + jnp.dot(pr.astype(BF16), v, preferred_element_type=F32)
        m_ref[idx] = m_new

    def process(get, tail):
        if tail:
            row = lax.broadcasted_iota(I32, (dec, LANES), 0)
            lane = lax.broadcasted_iota(I32, (dec, LANES), 1)
            ok = lane <= row
        for h in range(H_A):
            qh = (_cols(proj_ref, (AQ + h) * LANES, LANES) * SCALE).astype(BF16)
            kh = get("ak", h)
            vh = get("av", h)
            zero = jnp.zeros_like(qh)
            for mm in range(2):
                qm = jnp.where(lo, qh, zero) if mm == 0 else jnp.where(lo, zero, qh)
                s = lax.dot_general(qm, kh, NT, preferred_element_type=F32)
                if tail:
                    s = jnp.where(ok, s, NEG)
                update(ma, la, acca, 2 * h + mm, s, vh)
        for h in range(H_B):
            qh = (_cols(proj_ref, BQ * LANES + h * DH, DH) * SCALE).astype(BF16)
            s = lax.dot_general(qh, get("bk", h), NT, preferred_element_type=F32)
            s = s + cq_ref[0][:, h:h + 1] - csum_ref[0, h:h + 1, cols]
            if tail:
                s = jnp.where(ok, s, NEG)
            update(mb, lb, accb, h, s, get("bv", h))
        for h in range(H_C):
            qh = (_cols(proj_ref, CQ * LANES + h * DH, DH) * SCALE).astype(BF16)
            s = lax.dot_general(qh, get("ck", h), NT, preferred_element_type=F32)
            s = s + bias_ref[0, :, cols]
            update(mc, lc, accc, h, s, get("cv", h))

    cache = {"ak": ka_ref, "av": va_ref, "bk": kb_ref, "bv": vb_ref, "ck": kc_ref, "cv": vc_ref}
    newcol = {"ak": (AK, 2 * DH), "av": (AV, 2 * DH), "bk": (BK, DH), "bv": (BV, DH), "ck": (CK, DH), "cv": (CV, DH)}

    def get_page(name, h):
        return cache[name][0, 0, :, h, :].astype(BF16)

    def get_new(name, h):
        col, w = newcol[name]
        return _pad_rows(_cols(proj_ref, col * LANES + h * w, w).astype(BF16), LANES)

    @pl.when(p < pages)
    def _():
        process(get_page, False)

    @pl.when(p == pages)
    def _():
        process(get_new, True)
        lam, lam_init = _diff_lambda(lam_ref, layer)
        outs = []
        for h in range(H_A):
            o0 = acca[2 * h] / la[2 * h]
            o1 = acca[2 * h + 1] / la[2 * h + 1]
            outs.append(_diff_finish(o0, o1, lam, lam_init, g_ref[...]))
        oa_ref[...] = jnp.concatenate(outs, axis=1)
        ob_ref[...] = jnp.concatenate([accb[h] / lb[h] for h in range(H_B)], axis=1)
        oc_ref[...] = jnp.concatenate([accc[h] / lc[h] for h in range(H_C)], axis=1)


def _sample_attn(pt_flat, p32s, caches, csum_t, cq, bias, lam, g, layer, nseq, dec, pages):
    width = (pages + 1) * LANES

    def page_map5(b, p, pt):
        return (layer, pt[b * pages + jnp.minimum(p, pages - 1)], 0, 0, 0)

    seq_map = lambda b, p, pt: (b, 0, 0)
    in_specs = [pl.BlockSpec((dec, N_MAIN), lambda b, p, pt: (b, 0))]
    for c in caches:
        in_specs.append(pl.BlockSpec((1, 1) + c.shape[2:], page_map5))
    in_specs += [
        pl.BlockSpec((1, H_B, width), seq_map),
        pl.BlockSpec((1, dec, H_B), seq_map),
        pl.BlockSpec((1, dec, width), seq_map),
        pl.BlockSpec((4, DH), lambda b, p, pt: (0, 0)),
        pl.BlockSpec((1, 2 * DH), lambda b, p, pt: (0, 0)),
    ]
    st = lambda nst, w: [pltpu.VMEM((nst, dec, 1), F32), pltpu.VMEM((nst, dec, 1), F32), pltpu.VMEM((nst, dec, w), F32)]
    grid_spec = pltpu.PrefetchScalarGridSpec(
        num_scalar_prefetch=1,
        grid=(nseq, pages + 1),
        in_specs=in_specs,
        out_specs=[pl.BlockSpec((dec, 4 * LANES), lambda b, p, pt: (b, 0))] * 3,
        scratch_shapes=st(2 * H_A, 2 * DH) + st(H_B, DH) + st(H_C, DH),
    )
    return pl.pallas_call(
        functools.partial(_sample_attn_kernel, pages=pages, layer=layer, dec=dec),
        grid_spec=grid_spec,
        out_shape=[jax.ShapeDtypeStruct((nseq * dec, 4 * LANES), F32)] * 3,
        compiler_params=_cparams(("parallel", "arbitrary")),
        name="sample_attention",
    )(pt_flat, p32s, *caches, csum_t, cq, bias, lam, g)


def _layer_norm(h, g, b):
    mu = jnp.mean(h, axis=-1, keepdims=True)
    var = jnp.mean(jnp.square(h - mu), axis=-1, keepdims=True)
    return (h - mu) * lax.rsqrt(var + LN_EPS) * g + b


def _merge_kernel(oa_ref, ob_ref, oc_ref, g_ref, x_ref, wa_ref, wb_ref, wc_ref, wo_ref, lg_ref, lb_ref,
                  wr_ref, br_ref, x1_ref, x16_ref, gate_ref, *, alpha):
    ya = jnp.dot(oa_ref[...], wa_ref[...], preferred_element_type=F32)
    yb = jnp.dot(ob_ref[...], wb_ref[...], preferred_element_type=F32)
    yc = jnp.dot(oc_ref[...], wc_ref[...], preferred_element_type=F32)
    merged = (g_ref[:, 0:D_MODEL] * ya + g_ref[:, D_MODEL:2 * D_MODEL] * yb
              + g_ref[:, 2 * D_MODEL:3 * D_MODEL] * yc)
    mix = jnp.dot(merged.astype(BF16), wo_ref[...], preferred_element_type=F32)
    x1 = _layer_norm(alpha * x_ref[...] + mix, lg_ref[...], lb_ref[...])
    x1_ref[...] = x1
    x16_ref[...] = x1.astype(BF16)
    logits = jnp.dot(x1, wr_ref[...], precision=lax.Precision.HIGHEST, preferred_element_type=F32) + br_ref[...]
    lane = lax.broadcasted_iota(I32, logits.shape, 1)
    work = logits
    vals, hots = [], []
    for _ in range(TOP_K_EXPERTS):
        mx = jnp.max(work, axis=1, keepdims=True)
        ix = jnp.min(jnp.where(work == mx, lane, N_EXPERTS), axis=1, keepdims=True)
        hot = lane == ix
        vals.append(mx)
        hots.append(hot)
        work = jnp.where(hot, -jnp.inf, work)
    es = [jnp.exp(v - vals[0]) for v in vals]
    den = es[0] + es[1] + es[2] + es[3]
    gate = jnp.zeros(logits.shape, F32)
    for e, hot in zip(es, hots):
        gate = gate + jnp.where(hot, e / den, 0.0)
    gate_ref[...] = gate


def _merge(oa, ob, oc, gates, x, wa, wb, wc, wo, lg, lb, wr, br, tm, alpha):
    m = x.shape[0]
    row = lambda w: pl.BlockSpec((tm, w), lambda i: (i, 0))
    full = lambda a: pl.BlockSpec(a.shape, lambda i: (0, 0))
    return pl.pallas_call(
        functools.partial(_merge_kernel, alpha=alpha),
        grid=(m // tm,),
        in_specs=[row(4 * LANES), row(4 * LANES), row(4 * LANES), row(3 * D_MODEL), row(D_MODEL),
                  full(wa), full(wb), full(wc), full(wo), full(lg), full(lb), full(wr), full(br)],
        out_specs=[row(D_MODEL), row(D_MODEL), row(N_EXPERTS)],
        out_shape=[jax.ShapeDtypeStruct((m, D_MODEL), F32), jax.ShapeDtypeStruct((m, D_MODEL), BF16),
                   jax.ShapeDtypeStruct((m, N_EXPERTS), F32)],
        compiler_params=_cparams(("parallel",)),
        name="merge_ln_router",
    )(oa, ob, oc, gates, x, wa, wb, wc, wo, lg, lb, wr, br)


def _moe_kernel(x16_ref, x1_ref, gate_ref, w1g_ref, w1l_ref, b1g_ref, b1l_ref, w2_ref, b2_ref, lg_ref, lb_ref,
                o_ref, acc_scr, *, alpha):
    e = pl.program_id(1)

    @pl.when(e == 0)
    def _():
        acc_scr[...] = jnp.zeros(acc_scr.shape, F32)

    lane = lax.broadcasted_iota(I32, (1, N_EXPERTS), 1)
    ge = jnp.sum(jnp.where(lane == e, gate_ref[...], 0.0), axis=1, keepdims=True)
    x = x16_ref[...]
    glu = jnp.minimum(jnp.dot(x, w1g_ref[0], preferred_element_type=F32) + b1g_ref[0], SWIGLU_LIMIT)
    lin = jnp.clip(jnp.dot(x, w1l_ref[0], preferred_element_type=F32) + b1l_ref[0], -SWIGLU_LIMIT, SWIGLU_LIMIT)
    a = glu * jax.nn.sigmoid(SWIGLU_ALPHA * glu) * (lin + 1.0)
    y = jnp.dot(a.astype(BF16), w2_ref[0], preferred_element_type=F32) + b2_ref[0]
    acc_scr[...] += ge * y

    @pl.when(e == N_EXPERTS - 1)
    def _():
        o_ref[...] = _layer_norm(alpha * x1_ref[...] + acc_scr[...], lg_ref[...], lb_ref[...])


def _moe(x16, x1, gate, w1g, w1l, b1g, b1l, w2, b2, lg, lb, tm, alpha):
    m = x1.shape[0]
    d_ff = w1g.shape[2]
    row = lambda w: pl.BlockSpec((tm, w), lambda i, e: (i, 0))
    return pl.pallas_call(
        functools.partial(_moe_kernel, alpha=alpha),
        grid=(m // tm, N_EXPERTS),
        in_specs=[row(D_MODEL), row(D_MODEL), row(N_EXPERTS),
                  pl.BlockSpec((1, D_MODEL, d_ff), lambda i, e: (e, 0, 0)),
                  pl.BlockSpec((1, D_MODEL, d_ff), lambda i, e: (e, 0, 0)),
                  pl.BlockSpec((1, 1, d_ff), lambda i, e: (e, 0, 0)),
                  pl.BlockSpec((1, 1, d_ff), lambda i, e: (e, 0, 0)),
                  pl.BlockSpec((1, d_ff, D_MODEL), lambda i, e: (e, 0, 0)),
                  pl.BlockSpec((1, 1, D_MODEL), lambda i, e: (e, 0, 0)),
                  pl.BlockSpec((1, D_MODEL), lambda i, e: (0, 0)),
                  pl.BlockSpec((1, D_MODEL), lambda i, e: (0, 0))],
        out_specs=row(D_MODEL),
        out_shape=jax.ShapeDtypeStruct((m, D_MODEL), F32),
        scratch_shapes=[pltpu.VMEM((tm, D_MODEL), F32)],
        compiler_params=_cparams(("parallel", "arbitrary")),
        name="moe_ln",
    )(x16, x1, gate, w1g, w1l, b1g, b1l, w2, b2, lg, lb)


def _repack_w_in(w, b):
    offs = np.concatenate([[0], np.cumsum(PROJ_SIZES)])
    seg = lambda a, i: a[..., int(offs[i]):int(offs[i + 1])]
    d = w.shape[0]

    def build(a, rows):
        z = lambda n: jnp.zeros(rows + (n,), a.dtype)
        misc = jnp.concatenate([z(MISC_BF), seg(a, 6), seg(a, 12), z(LANES - MISC_IW - H_I)], axis=-1)
        main = jnp.concatenate([seg(a, 0), seg(a, 1), seg(a, 2), seg(a, 3), seg(a, 4), seg(a, 5),
                                seg(a, 7), seg(a, 8), seg(a, 9), seg(a, 10), seg(a, 11), seg(a, 11), misc], axis=-1)
        return main, seg(a, 13)

    wm, wg = build(w, (d,))
    bm, bg = build(b[None, :], (1,))
    return wm.astype(BF16), bm, wg.astype(BF16), bg


def _rope_tables(pos):
    half = ROT_DIM // 2
    inv = ROPE_THETA ** (-jnp.arange(half, dtype=F32) / half)
    ang = pos.astype(F32)[:, None] * inv
    cos, sin = jnp.cos(ang), jnp.sin(ang)
    m = pos.shape[0]
    ones = jnp.ones((m, DH - ROT_DIM), F32)
    zeros = jnp.zeros((m, DH - ROT_DIM), F32)
    z8 = jnp.zeros((m, half), F32)
    c64 = jnp.concatenate([cos, cos, ones], axis=1)
    a64 = jnp.concatenate([-sin, z8, zeros], axis=1)
    b64 = jnp.concatenate([z8, sin, zeros], axis=1)
    dup = lambda a: jnp.concatenate([a, a], axis=1)
    return dup(c64), dup(a64), dup(b64)


def kernel(x_prompt, x_sample, cache_a_k, cache_a_v, cache_b_k, cache_b_v, cache_b_logf, cache_c_k, cache_c_v,
           cache_c_idx_k, page_table, meta, w_in, b_in, a_lambda, a_norm_g, w_br_a, w_br_b, w_br_c, w_out,
           ln1_g, ln1_b, ln2_g, ln2_b, w_router, b_router, w_mlp1, b_mlp1, w_mlp2, b_mlp2):
    n_p, seq, _ = x_prompt.shape
    nseq, dec, _ = x_sample.shape
    depth = w_in.shape[0]
    pages = page_table.shape[1]
    past_len = pages * PAGE
    t = seq + BLOCK_Q
    topk_prompt = min(TOPK_MAX, seq // 4)
    topk_sample = min(TOPK_MAX, (past_len + dec) // 4)
    alpha = (2 * depth) ** 0.25
    mp = n_p * t
    m_all = mp + nseq * dec
    tm = _pick(m_all, (768, 640, 512, 384, 256, 128, 64, 32, 16, 8))

    xp = jnp.concatenate([jnp.zeros((n_p, META_PAD, D_MODEL), F32),
                          jnp.broadcast_to(meta[None], (n_p, N_META, D_MODEL)), x_prompt], axis=1)
    x = jnp.concatenate([xp.reshape(mp, D_MODEL), x_sample.reshape(nseq * dec, D_MODEL)], axis=0)
    pos = jnp.concatenate([jnp.tile(jnp.arange(t, dtype=I32) - META_PAD, n_p),
                           jnp.tile(past_len + jnp.arange(dec, dtype=I32), nseq)])
    cos, sa, sb = _rope_tables(pos)
    pt_flat = page_table.reshape(-1).astype(I32)
    caches = (cache_a_k, cache_a_v, cache_b_k, cache_b_v, cache_c_k, cache_c_v)

    rows_p, rows_s = [], []
    for layer in range(depth):
        wm, bm, wg, bg = _repack_w_in(w_in[layer], b_in[layer])
        x16 = x.astype(BF16)
        p32, p16 = _project_main(x16, wm, bm, cos, sa, sb, tm)
        gates = _project_gates(x16, wg, bg, tm)
        lam = a_lambda[layer]
        g = a_norm_g[layer][None, :]

        bf_t = jnp.swapaxes(p32[:mp, MISC * LANES + MISC_BF:MISC * LANES + MISC_BF + H_B].reshape(n_p, t, H_B), 1, 2)
        logf_t, csum_t = _prompt_cumsum(bf_t)
        bias = _prompt_index(p16, p32, n_p, t, topk_prompt)
        oa = _prompt_flash("A", p16, (lam, g), n_p, t, layer)
        ob = _prompt_flash("B", p16, (jnp.swapaxes(csum_t, 1, 2), csum_t), n_p, t, layer)
        oc = _prompt_flash("C", p16, (bias,), n_p, t, layer)

        p32s = p32[mp:]
        sbias, scsum, scq, slogf = _sample_pre(pt_flat, p32s, cache_c_idx_k, cache_b_logf, layer, nseq, dec,
                                               pages, topk_sample)
        soa, sob, soc = _sample_attn(pt_flat, p32s, caches, scsum, scq, sbias, lam, g, layer, nseq, dec, pages)

        oa = jnp.concatenate([oa, soa.astype(BF16)], axis=0)
        ob = jnp.concatenate([ob, sob.astype(BF16)], axis=0)
        oc = jnp.concatenate([oc, soc.astype(BF16)], axis=0)
        x1, x1_16, gate = _merge(oa, ob, oc, gates, x,
                                 w_br_a[layer].astype(BF16), w_br_b[layer].astype(BF16), w_br_c[layer].astype(BF16),
                                 w_out[layer].astype(BF16), ln1_g[layer][None], ln1_b[layer][None],
                                 w_router[layer], b_router[layer][None], tm, alpha)
        w1 = w_mlp1[layer]
        x = _moe(x1_16, x1, gate,
                 w1[:, :, 0::2].astype(BF16), w1[:, :, 1::2].astype(BF16),
                 b_mlp1[layer][:, None, 0::2], b_mlp1[layer][:, None, 1::2],
                 w_mlp2[layer].astype(BF16), b_mlp2[layer][:, None, :],
                 ln2_g[layer][None], ln2_b[layer][None], tm, alpha)

        def seg(rows, col, width, shape):
            return rows[:, col * LANES:col * LANES + width].reshape(shape)

        pp = p32[:mp].reshape(n_p, t, N_MAIN)[:, META_PAD:].reshape(n_p * (t - META_PAD), N_MAIN)
        tp = t - META_PAD
        rows_p.append((seg(pp, AK, W_A, (n_p, tp, H_A, 2 * DH)), seg(pp, AV, W_A, (n_p, tp, H_A, 2 * DH)),
                       seg(pp, BK, W_B, (n_p, tp, H_B, DH)), seg(pp, BV, W_B, (n_p, tp, H_B, DH)),
                       jnp.swapaxes(logf_t, 1, 2)[:, META_PAD:],
                       seg(pp, CK, W_C, (n_p, tp, H_C, DH)), seg(pp, CV, W_C, (n_p, tp, H_C, DH)),
                       seg(pp, IK, D_IDX, (n_p, tp, D_IDX))))
        rows_s.append((seg(p32s, AK, W_A, (nseq, dec, H_A, 2 * DH)), seg(p32s, AV, W_A, (nseq, dec, H_A, 2 * DH)),
                       seg(p32s, BK, W_B, (nseq, dec, H_B, DH)), seg(p32s, BV, W_B, (nseq, dec, H_B, DH)),
                       slogf,
                       seg(p32s, CK, W_C, (nseq, dec, H_C, DH)), seg(p32s, CV, W_C, (nseq, dec, H_C, DH)),
                       seg(p32s, IK, D_IDX, (nseq, dec, D_IDX))))

    st = lambda rows, i: jnp.stack([r[i] for r in rows])
    y_prompt = x[:mp].reshape(n_p, t, D_MODEL)[:, BLOCK_Q:]
    y_sample = x[mp:].reshape(nseq, dec, D_MODEL)
    return (y_prompt, y_sample) + tuple(st(rows_p, i) for i in range(8)) + tuple(st(rows_s, i) for i in range(8))
```

```python
import functools
import math

import numpy as np
import jax
import jax.numpy as jnp
from jax import lax
from jax.experimental import pallas as pl
from jax.experimental.pallas import tpu as pltpu

F32 = jnp.float32
BF16 = jnp.bfloat16
I32 = jnp.int32

D_MODEL = 1024
DH = 64
H_A = 4
H_B = 8
H_C = 8
H_I = 4
D_IDX = 64
TOPK_MAX = 256
N_META = 16
BLOCK_Q = 128
META_PAD = BLOCK_Q - N_META
ROT_DIM = DH // 4
ROPE_THETA = 500000.0
N_EXPERTS = 32
TOP_K_EXPERTS = 4
SWIGLU_ALPHA = 1.702
SWIGLU_LIMIT = 7.0
LN_EPS = 1e-5
RMS_EPS = 1e-5
NEG = -1e30
M_INIT = float(np.finfo(np.float32).min)
PAGE = 128
LANES = 128
W_A = H_A * 2 * DH
W_B = H_B * DH
W_C = H_C * DH
PROJ_SIZES = (W_A, W_A, W_A, W_B, W_B, W_B, H_B, W_C, W_C, W_C, H_I * D_IDX, D_IDX, H_I, 3 * D_MODEL)
SCALE = DH ** -0.5
IW_SCALE = H_I ** -0.5 * D_IDX ** -0.5

AQ, AK, AV, BQ, BK, BV, CQ, CK, CV, IQ, IK, MISC = 0, 4, 8, 12, 16, 20, 24, 28, 32, 36, 38, 39
N_MAIN = 40 * LANES
MISC_BF = 16
MISC_IW = 24
PROJ_TN = 512
ROPE_TILES = (1, 1, 0, 0, 0, 0, 1, 1, 0, 1)
VMEM_LIMIT = 56 * 1024 * 1024
INT_MIN = -2 ** 31
FLASH_STRIPS = 1

NT = (((1,), (1,)), ((), ()))


def _pick(n, cands):
    for c in cands:
        if n % c == 0:
            return c
    raise ValueError(f"no tile for {n}")


def _cparams(sem):
    return pltpu.CompilerParams(dimension_semantics=sem, vmem_limit_bytes=VMEM_LIMIT)


def _log_sigmoid(x):
    return jnp.minimum(x, 0.0) - jnp.log1p(jnp.exp(-jnp.abs(x)))


def _lane_scan(x):
    lane = lax.broadcasted_iota(I32, x.shape, 1)
    s = 1
    while s < LANES:
        x = x + jnp.where(lane >= s, pltpu.roll(x, s, 1), 0.0)
        s *= 2
    return x


def _rope_tile(x, cos, sa, sb):
    outs = []
    for c in range(x.shape[1] // LANES):
        xc = x[:, c * LANES:(c + 1) * LANES]
        up = pltpu.roll(xc, LANES - ROT_DIM // 2, 1)
        dn = pltpu.roll(xc, ROT_DIM // 2, 1)
        outs.append(xc * cos + up * sa + dn * sb)
    return jnp.concatenate(outs, axis=1)


def _proj_kernel(flags_ref, x_ref, w_ref, b_ref, cos_ref, sa_ref, sb_ref, o32_ref, o16_ref):
    j = pl.program_id(1)
    acc = jnp.dot(x_ref[...], w_ref[...], preferred_element_type=F32) + b_ref[...]

    @pl.when(flags_ref[j] == 0)
    def _():
        o32_ref[...] = acc
        o16_ref[...] = acc.astype(BF16)

    @pl.when(flags_ref[j] == 1)
    def _():
        r = _rope_tile(acc, cos_ref[...], sa_ref[...], sb_ref[...])
        o32_ref[...] = r
        o16_ref[...] = r.astype(BF16)


def _project_main(x16, w, b, cos, sa, sb, tm):
    m = x16.shape[0]
    n = w.shape[1]
    flags = jnp.asarray(ROPE_TILES, I32)
    grid_spec = pltpu.PrefetchScalarGridSpec(
        num_scalar_prefetch=1,
        grid=(m // tm, n // PROJ_TN),
        in_specs=[
            pl.BlockSpec((tm, D_MODEL), lambda i, j, f: (i, 0)),
            pl.BlockSpec((D_MODEL, PROJ_TN), lambda i, j, f: (0, j)),
            pl.BlockSpec((1, PROJ_TN), lambda i, j, f: (0, j)),
            pl.BlockSpec((tm, LANES), lambda i, j, f: (i, 0)),
            pl.BlockSpec((tm, LANES), lambda i, j, f: (i, 0)),
            pl.BlockSpec((tm, LANES), lambda i, j, f: (i, 0)),
        ],
        out_specs=[
            pl.BlockSpec((tm, PROJ_TN), lambda i, j, f: (i, j)),
            pl.BlockSpec((tm, PROJ_TN), lambda i, j, f: (i, j)),
        ],
    )
    return pl.pallas_call(
        _proj_kernel,
        grid_spec=grid_spec,
        out_shape=[jax.ShapeDtypeStruct((m, n), F32), jax.ShapeDtypeStruct((m, n), BF16)],
        compiler_params=_cparams(("parallel", "arbitrary")),
        name="proj_main",
    )(flags, x16, w, b, cos, sa, sb)


def _gate_kernel(x_ref, w_ref, b_ref, o_ref):
    acc = jnp.dot(x_ref[...], w_ref[...], preferred_element_type=F32) + b_ref[...]
    o_ref[...] = jax.nn.sigmoid(acc)


def _project_gates(x16, w, b, tm):
    m = x16.shape[0]
    n = w.shape[1]
    return pl.pallas_call(
        _gate_kernel,
        grid=(m // tm, n // PROJ_TN),
        in_specs=[
            pl.BlockSpec((tm, D_MODEL), lambda i, j: (i, 0)),
            pl.BlockSpec((D_MODEL, PROJ_TN), lambda i, j: (0, j)),
            pl.BlockSpec((1, PROJ_TN), lambda i, j: (0, j)),
        ],
        out_specs=pl.BlockSpec((tm, PROJ_TN), lambda i, j: (i, j)),
        out_shape=jax.ShapeDtypeStruct((m, n), F32),
        compiler_params=_cparams(("parallel", "arbitrary")),
        name="proj_gates",
    )(x16, w, b)


def _cumsum_kernel(bf_ref, logf_ref, csum_ref, *, t):
    def body(c, carry):
        sl = pl.ds(pl.multiple_of(c * LANES, LANES), LANES)
        tok = c * LANES + lax.broadcasted_iota(I32, (H_B, LANES), 1)
        lf = jnp.where(tok >= META_PAD, _log_sigmoid(bf_ref[0, :, sl]), 0.0)
        logf_ref[0, :, sl] = lf
        cs = _lane_scan(lf) + carry
        csum_ref[0, :, sl] = cs
        return cs[:, LANES - 1:LANES]

    lax.fori_loop(0, t // LANES, body, jnp.zeros((H_B, 1), F32))


def _prompt_cumsum(bf_t):
    n, h, t = bf_t.shape
    return pl.pallas_call(
        functools.partial(_cumsum_kernel, t=t),
        grid=(n,),
        in_specs=[pl.BlockSpec((1, h, t), lambda b: (b, 0, 0))],
        out_specs=[pl.BlockSpec((1, h, t), lambda b: (b, 0, 0)),
                   pl.BlockSpec((1, h, t), lambda b: (b, 0, 0))],
        out_shape=[jax.ShapeDtypeStruct((n, h, t), F32), jax.ShapeDtypeStruct((n, h, t), F32)],
        compiler_params=_cparams(("parallel",)),
        name="prompt_cumsum",
    )(bf_t)


def _topk_select(sc_ref, key_ref, out_ref, nchunks, nsuper, cw, k, rows, out_dtype):
    def sl_of(c):
        return pl.ds(pl.multiple_of(c * LANES, LANES), LANES)

    def make_keys(c, _):
        bits = pltpu.bitcast(sc_ref[:, sl_of(c)], I32)
        key_ref[:, sl_of(c)] = jnp.where(bits < 0, bits ^ jnp.int32(0x7FFFFFFF), bits)
        return 0

    lax.fori_loop(0, nsuper * cw, make_keys, 0)

    def count(pred):
        def body(sc, acc):
            for jj in range(cw):
                acc = acc + jnp.where(pred(key_ref[:, sl_of(sc * cw + jj)]), 1, 0).astype(I32)
            return acc
        acc = lax.fori_loop(0, nsuper, body, jnp.zeros((rows, LANES), I32))
        return jnp.sum(acc, axis=1, keepdims=True)

    c0 = count(lambda key: key >= 0)
    t0 = jnp.where(c0 >= k, jnp.int32(0), jnp.int32(INT_MIN))

    def bit_body(i, t):
        cand = t | jnp.left_shift(jnp.int32(1), 30 - i)
        cnt = count(lambda key: key >= cand)
        return jnp.where(cnt >= k, cand, t)

    thr = lax.fori_loop(0, 31, bit_body, t0)
    need = (k - count(lambda key: key > thr)).astype(F32)
    upper = (lax.broadcasted_iota(I32, (LANES, LANES), 0)
             < lax.broadcasted_iota(I32, (LANES, LANES), 1)).astype(BF16)

    def sel_body(c, carry):
        key = key_ref[:, sl_of(c)]
        eq = key == thr
        eqf = jnp.where(eq, 1.0, 0.0)
        rank = jnp.dot(eqf.astype(BF16), upper, preferred_element_type=F32) + carry
        sel = (key > thr) | (eq & (rank < need))
        valid = sc_ref[:, sl_of(c)] > 0.5 * NEG
        out_ref[:, sl_of(c)] = jnp.where(sel & valid, 0.0, NEG).astype(out_dtype)
        return carry + jnp.sum(eqf, axis=1, keepdims=True)

    lax.fori_loop(0, nchunks, sel_body, jnp.zeros((rows, 1), F32))


def _index_scores(iq_heads, iw, kmat):
    acc = None
    for h in range(H_I):
        s = lax.dot_general(iq_heads[h], kmat, NT, preferred_element_type=F32)
        term = jnp.maximum(s, 0.0) * iw[:, h:h + 1]
        acc = term if acc is None else acc + term
    return acc


def _prompt_index_kernel(iq_ref, ik_ref, misc_ref, bias_ref, sc_scr, key_scr, *, k, cw):
    i = pl.program_id(1)
    nchunks = i + 1
    lo = lax.broadcasted_iota(I32, (1, LANES), 1) < DH
    zero = jnp.zeros((BLOCK_Q, LANES), BF16)
    iq_heads = []
    for pair in range(H_I // 2):
        blk = iq_ref[:, pair * LANES:(pair + 1) * LANES]
        iq_heads.append(jnp.where(lo, blk, zero))
        iq_heads.append(jnp.where(lo, zero, blk))
    iw = misc_ref[...][:, MISC_IW:MISC_IW + H_I] * IW_SCALE
    qidx = i * BLOCK_Q + lax.broadcasted_iota(I32, (BLOCK_Q, 1), 0)

    def chunk(c, _):
        sl = pl.ds(pl.multiple_of(c * LANES, LANES), LANES)
        acc = _index_scores(iq_heads, iw, ik_ref[sl, :])
        kidx = c * LANES + lax.broadcasted_iota(I32, (1, LANES), 1)
        ok = (kidx <= qidx) & (kidx >= META_PAD)
        sc_scr[:, sl] = jnp.where(ok, acc, NEG)
        return 0

    lax.fori_loop(0, nchunks, chunk, 0)
    nsuper = (nchunks + cw - 1) // cw

    def fill(c, _):
        sc_scr[:, pl.ds(pl.multiple_of(c * LANES, LANES), LANES)] = jnp.full((BLOCK_Q, LANES), NEG, F32)
        return 0

    lax.fori_loop(nchunks, nsuper * cw, fill, 0)
    bias_ref[...] = jnp.full(bias_ref.shape, NEG, bias_ref.dtype)
    _topk_select(sc_scr, key_scr, bias_ref.at[0], nchunks, nsuper, cw, k, BLOCK_Q, bias_ref.dtype)


def _prompt_index(p16, p32, n, t, k):
    nq = t // BLOCK_Q
    return pl.pallas_call(
        functools.partial(_prompt_index_kernel, k=k, cw=_pick(t, (640, 512, 384, 256, 128)) // LANES),
        grid=(n, nq),
        in_specs=[
            pl.BlockSpec((BLOCK_Q, 2 * LANES), lambda b, i: (b * nq + i, IQ // 2)),
            pl.BlockSpec((t, LANES), lambda b, i: (b, IK)),
            pl.BlockSpec((BLOCK_Q, LANES), lambda b, i: (b * nq + i, MISC)),
        ],
        out_specs=pl.BlockSpec((1, BLOCK_Q, t), lambda b, i: (b, i, 0)),
        out_shape=jax.ShapeDtypeStruct((n, t, t), BF16),
        scratch_shapes=[pltpu.VMEM((BLOCK_Q, t), F32), pltpu.VMEM((BLOCK_Q, t), I32)],
        compiler_params=_cparams(("parallel", "arbitrary")),
        name="prompt_index_topk",
    )(p16, p16, p32)


def _diff_lambda(lam_ref, layer):
    lam_init = 0.8 - 0.6 * math.exp(-0.3 * layer)
    lv = lam_ref[...]
    a = jnp.sum(lv[0:1] * lv[1:2], axis=1, keepdims=True)
    b = jnp.sum(lv[2:3] * lv[3:4], axis=1, keepdims=True)
    return jnp.exp(a) - jnp.exp(b) + lam_init, lam_init


def _diff_finish(o0, o1, lam, lam_init, g):
    o = o0 - lam * o1
    o = o * lax.rsqrt(jnp.mean(jnp.square(o), axis=-1, keepdims=True) + RMS_EPS)
    return o * g * (1.0 - lam_init)


def _flash_kernel(qi_ref, kj_ref, *refs, mode, tq, tk, layer):
    if mode == "A":
        q_ref, k_ref, v_ref, lam_ref, g_ref, o_ref, m_scr, l_scr, acc_scr = refs
    elif mode == "B":
        q_ref, k_ref, v_ref, cq_ref, ck_ref, o_ref, m_scr, l_scr, acc_scr, cq_scr = refs
    else:
        q_ref, k_ref, v_ref, bias_ref, o_ref, m_scr, l_scr, acc_scr = refs
    cb = pl.program_id(1)
    step = pl.program_id(2)
    i = qi_ref[step]
    j = kj_ref[step]
    lo = lax.broadcasted_iota(I32, (1, LANES), 1) < DH

    @pl.when(j == 0)
    def _():
        m_scr[...] = jnp.full(m_scr.shape, M_INIT, F32)
        l_scr[...] = jnp.zeros(l_scr.shape, F32)
        acc_scr[...] = jnp.zeros(acc_scr.shape, F32)
        if mode == "B":
            lane8 = lax.broadcasted_iota(I32, (1, H_B), 1)
            for u in range(2):
                cq_scr[u] = jnp.sum(jnp.where(lane8 == 2 * cb + u, cq_ref[0], 0.0), axis=1, keepdims=True)

    def update(masked):
        sr = tq // FLASH_STRIPS

        def strip(r, carry):
            rows = pl.ds(pl.multiple_of(r * sr, sr), sr)
            q = q_ref[rows, :] * jnp.asarray(SCALE, BF16)
            zero = jnp.zeros_like(q)
            k = k_ref[...]
            v = v_ref[...]
            if masked:
                qidx = i * tq + r * sr + lax.broadcasted_iota(I32, (sr, 1), 0)
                kidx = j * tk + lax.broadcasted_iota(I32, (1, tk), 1)
                ok = (kidx <= qidx) & (kidx >= META_PAD)
            for u in range(2):
                qu = jnp.where(lo, q, zero) if u == 0 else jnp.where(lo, zero, q)
                s = lax.dot_general(qu, k, NT, preferred_element_type=F32)
                if mode == "B":
                    s = s + cq_scr[u, rows, :] - ck_ref[0, pl.ds(2 * cb + u, 1), :]
                if mode == "C":
                    s = s + bias_ref[0, rows, :].astype(F32)
                if masked:
                    s = jnp.where(ok, s, NEG)
                m_prev = m_scr[u, rows, :]
                m_new = jnp.maximum(m_prev, jnp.max(s, axis=1, keepdims=True))
                alpha = jnp.exp(m_prev - m_new)
                p = jnp.exp(s - m_new)
                l_scr[u, rows, :] = alpha * l_scr[u, rows, :] + jnp.sum(p, axis=1, keepdims=True)
                acc_scr[u, rows, :] = alpha * acc_scr[u, rows, :] + jnp.dot(p.astype(BF16), v,
                                                                            preferred_element_type=F32)
                m_scr[u, rows, :] = m_new
            return carry

        lax.fori_loop(0, FLASH_STRIPS, strip, 0)

    if mode == "C":
        update(False)
    else:
        edge = (j == i) | (j == 0)

        @pl.when(edge)
        def _():
            update(True)

        @pl.when(jnp.logical_not(edge))
        def _():
            update(False)

    @pl.when(j == i)
    def _():
        o0 = acc_scr[0] / l_scr[0]
        o1 = acc_scr[1] / l_scr[1]
        if mode == "A":
            lam, lam_init = _diff_lambda(lam_ref, layer)
            o_ref[...] = _diff_finish(o0, o1, lam, lam_init, g_ref[...]).astype(o_ref.dtype)
        else:
            o_ref[...] = jnp.where(lo, o0, o1).astype(o_ref.dtype)


def _prompt_flash(mode, p16, extras, n, t, layer):
    tq = _pick(t, (640, 512, 384, 256, 128))
    tk = tq
    nq = t // tq
    qi = np.array([i for i in range(nq) for _ in range(i + 1)], np.int32)
    kj = np.array([j for i in range(nq) for j in range(i + 1)], np.int32)
    qcol, kcol, vcol = {"A": (AQ, AK, AV), "B": (BQ, BK, BV), "C": (CQ, CK, CV)}[mode]
    in_specs = [
        pl.BlockSpec((tq, LANES), lambda b, c, s, qi, kj: (b * nq + qi[s], qcol + c)),
        pl.BlockSpec((tk, LANES), lambda b, c, s, qi, kj: (b * nq + kj[s], kcol + c)),
        pl.BlockSpec((tk, LANES), lambda b, c, s, qi, kj: (b * nq + kj[s], vcol + c)),
    ]
    scratch = [pltpu.VMEM((2, tq, 1), F32), pltpu.VMEM((2, tq, 1), F32), pltpu.VMEM((2, tq, LANES), F32)]
    if mode == "A":
        in_specs += [pl.BlockSpec((4, DH), lambda b, c, s, qi, kj: (0, 0)),
                     pl.BlockSpec((1, 2 * DH), lambda b, c, s, qi, kj: (0, 0))]
    elif mode == "B":
        in_specs += [pl.BlockSpec((1, tq, H_B), lambda b, c, s, qi, kj: (b, qi[s], 0)),
                     pl.BlockSpec((1, H_B, tk), lambda b, c, s, qi, kj: (b, 0, kj[s]))]
        scratch += [pltpu.VMEM((2, tq, 1), F32)]
    else:
        in_specs += [pl.BlockSpec((1, tq, tk), lambda b, c, s, qi, kj: (b, qi[s], kj[s]))]
    grid_spec = pltpu.PrefetchScalarGridSpec(
        num_scalar_prefetch=2,
        grid=(n, 4, len(qi)),
        in_specs=in_specs,
        out_specs=pl.BlockSpec((tq, LANES), lambda b, c, s, qi, kj: (b * nq + qi[s], c)),
        scratch_shapes=scratch,
    )
    return pl.pallas_call(
        functools.partial(_flash_kernel, mode=mode, tq=tq, tk=tk, layer=layer),
        grid_spec=grid_spec,
        out_shape=jax.ShapeDtypeStruct((n * t, 4 * LANES), BF16),
        compiler_params=_cparams(("parallel", "parallel", "arbitrary")),
        name=f"prompt_flash_{mode}",
    )(jnp.asarray(qi), jnp.asarray(kj), p16, p16, p16, *extras)


def _pad_rows(x, rows):
    return jnp.concatenate([x, jnp.zeros((rows - x.shape[0], x.shape[1]), x.dtype)], axis=0)


def _cols(ref, start, width):
    blk = start // LANES
    off = start - blk * LANES
    x = ref[:, blk * LANES:(blk + 1) * LANES]
    return x if width == LANES else x[:, off:off + width]


def _stack_rows(x, reps):
    return jnp.concatenate([x] * reps, axis=0)


def _spread_rows(x, reps):
    return jnp.concatenate([jnp.broadcast_to(x[h:h + 1], (reps, x.shape[1])) for h in range(x.shape[0])], axis=0)


def _sample_pre_kernel(pt_ref, proj_ref, ikt_ref, lft_ref, bias_ref, csum_ref, cq_ref, lnew_ref,
                       sc_scr, key_scr, carry_scr, *, pages, k, dec):
    p = pl.program_id(1)
    iq = jnp.concatenate([_cols(proj_ref, IQ * LANES + h * D_IDX, D_IDX) for h in range(H_I)], axis=0).astype(BF16)
    iw = _cols(proj_ref, MISC * LANES + MISC_IW, H_I) * IW_SCALE
    iw_rows = jnp.concatenate([iw[:, h:h + 1] for h in range(H_I)], axis=0)
    cols = pl.ds(pl.multiple_of(p * LANES, LANES), LANES)

    def scores(s):
        w = jnp.maximum(s, 0.0) * iw_rows
        out = w[0:dec]
        for h in range(1, H_I):
            out = out + w[h * dec:(h + 1) * dec]
        return out

    @pl.when(p == 0)
    def _():
        carry_scr[...] = jnp.zeros(carry_scr.shape, F32)

    def put_csum(lt):
        cs = _lane_scan(lt) + carry_scr[...]
        csum_ref[0, :, cols] = cs
        carry_scr[...] = cs[:, LANES - 1:LANES]
        return cs

    @pl.when(p < pages)
    def _():
        sc_scr[:, cols] = scores(jnp.dot(iq, ikt_ref[0, 0].astype(BF16), preferred_element_type=F32))
        put_csum(lft_ref[0, 0])

    @pl.when(p == pages)
    def _():
        knew = _pad_rows(_cols(proj_ref, IK * LANES, D_IDX).astype(BF16), LANES)
        s = scores(lax.dot_general(iq, knew, NT, preferred_element_type=F32))
        row = lax.broadcasted_iota(I32, (dec, LANES), 0)
        lane = lax.broadcasted_iota(I32, (dec, LANES), 1)
        sc_scr[:, cols] = jnp.where(lane <= row, s, NEG)
        lnew = _log_sigmoid(_cols(proj_ref, MISC * LANES + MISC_BF, H_B))
        lnew_ref[0] = lnew
        eye = (lax.broadcasted_iota(I32, (H_B, H_B), 0) == lax.broadcasted_iota(I32, (H_B, H_B), 1)).astype(F32)
        lt = lax.dot_general(eye, _pad_rows(lnew, LANES), NT, precision=lax.Precision.HIGHEST,
                             preferred_element_type=F32)
        cs = put_csum(lt)
        pick = (row == lane).astype(F32)
        cq_ref[0] = lax.dot_general(pick, cs, NT, precision=lax.Precision.HIGHEST, preferred_element_type=F32)
        _topk_select(sc_scr, key_scr, bias_ref.at[0], pages + 1, 1, pages + 1, k, dec, F32)


def _sample_pre(pt_flat, p32s, ikt, lft, layer, nseq, dec, pages, k):
    width = (pages + 1) * LANES

    def page_map(b, p, pt):
        return (layer, pt[b * pages + jnp.minimum(p, pages - 1)], 0, 0)

    grid_spec = pltpu.PrefetchScalarGridSpec(
        num_scalar_prefetch=1,
        grid=(nseq, pages + 1),
        in_specs=[
            pl.BlockSpec((dec, N_MAIN), lambda b, p, pt: (b, 0)),
            pl.BlockSpec((1, 1, D_IDX, PAGE), page_map),
            pl.BlockSpec((1, 1, H_B, PAGE), page_map),
        ],
        out_specs=[
            pl.BlockSpec((1, dec, width), lambda b, p, pt: (b, 0, 0)),
            pl.BlockSpec((1, H_B, width), lambda b, p, pt: (b, 0, 0)),
            pl.BlockSpec((1, dec, H_B), lambda b, p, pt: (b, 0, 0)),
            pl.BlockSpec((1, dec, H_B), lambda b, p, pt: (b, 0, 0)),
        ],
        scratch_shapes=[pltpu.VMEM((dec, width), F32), pltpu.VMEM((dec, width), I32), pltpu.VMEM((H_B, 1), F32)],
    )
    return pl.pallas_call(
        functools.partial(_sample_pre_kernel, pages=pages, k=k, dec=dec),
        grid_spec=grid_spec,
        out_shape=[jax.ShapeDtypeStruct((nseq, dec, width), F32),
                   jax.ShapeDtypeStruct((nseq, H_B, width), F32),
                   jax.ShapeDtypeStruct((nseq, dec, H_B), F32),
                   jax.ShapeDtypeStruct((nseq, dec, H_B), F32)],
        compiler_params=_cparams(("parallel", "arbitrary")),
        name="sample_index_topk",
    )(pt_flat, p32s, ikt, lft)


def _sample_attn_kernel(pt_ref, proj_ref, ka_ref, va_ref, kb_ref, vb_ref, kc_ref, vc_ref,
                        csum_ref, cq_ref, bias_ref, lam_ref, g_ref,
                        oa_ref, ob_ref, oc_ref,
                        qa_scr, qb_scr, qc_scr, cqb_scr, ma, la, acca, mb, lb, accb, mc, lc, accc,
                        *, pages, layer, dec):
    p = pl.program_id(1)
    wide = H_B * DH
    lo = lax.broadcasted_iota(I32, (1, LANES), 1) < DH
    colhead = lax.broadcasted_iota(I32, (1, wide), 1) // DH
    cols = pl.ds(pl.multiple_of(p * LANES, LANES), LANES)
    states = ((ma, la, acca), (mb, lb, accb), (mc, lc, accc))

    @pl.when(p == 0)
    def _():
        for m_ref, l_ref, acc_ref in states:
            m_ref[...] = jnp.full(m_ref.shape, M_INIT, F32)
            l_ref[...] = jnp.zeros(l_ref.shape, F32)
            acc_ref[...] = jnp.zeros(acc_ref.shape, F32)
        pieces = []
        for h in range(H_A):
            qh = _cols(proj_ref, (AQ + h) * LANES, LANES) * SCALE
            pieces += [jnp.where(lo, qh, 0.0), jnp.where(lo, 0.0, qh)]
        qa_scr[...] = jnp.concatenate(pieces, axis=0).astype(BF16)
        for q_scr, col in ((qb_scr, BQ), (qc_scr, CQ)):
            q_all = proj_ref[:, col * LANES:col * LANES + wide] * SCALE
            q_scr[...] = jnp.concatenate([jnp.where(colhead == h, q_all, 0.0) for h in range(H_B)],
                                         axis=0).astype(BF16)
        cq = cq_ref[0]
        cqb_scr[...] = jnp.concatenate([cq[:, h:h + 1] for h in range(H_B)], axis=0)

    def online(state, rows, s, pv):
        m_ref, l_ref, acc_ref = state
        m_prev = m_ref[rows, :]
        m_new = jnp.maximum(m_prev, jnp.max(s, axis=1, keepdims=True))
        alpha = jnp.exp(m_prev - m_new)
        pr = jnp.exp(s - m_new)
        l_ref[rows, :] = alpha * l_ref[rows, :] + jnp.sum(pr, axis=1, keepdims=True)
        acc_ref[rows, :] = alpha * acc_ref[rows, :] + pv(pr.astype(BF16))
        m_ref[rows, :] = m_new

    everything = slice(None)
    b_bias = lambda: cqb_scr[...] - _spread_rows(csum_ref[0, :, cols], dec)
    c_bias = lambda: _stack_rows(bias_ref[0, :, cols], H_C)

    @pl.when(p < pages)
    def _():
        ka = ka_ref[0, 0].astype(BF16)
        sa = lax.dot_general(qa_scr[...], ka, NT, preferred_element_type=F32)
        rowh = lax.broadcasted_iota(I32, (2 * H_A * dec, 1), 0) // (2 * dec)
        colh = lax.broadcasted_iota(I32, (1, H_A * PAGE), 1) % H_A
        sa = jnp.where(colh == rowh, sa, NEG)
        online(states[0], everything, sa,
               lambda pr: jnp.dot(pr, va_ref[0, 0].astype(BF16), preferred_element_type=F32))
        for state, q_scr, k_ref, v_ref, bias in ((states[1], qb_scr, kb_ref, vb_ref, b_bias),
                                                 (states[2], qc_scr, kc_ref, vc_ref, c_bias)):
            kt = k_ref[0, 0].reshape(wide, PAGE).astype(BF16)
            s = jnp.dot(q_scr[...], kt, preferred_element_type=F32) + bias()
            online(state, everything, s,
                   lambda pr, v_ref=v_ref: lax.dot_general(pr, v_ref[0, 0].reshape(wide, PAGE).astype(BF16), NT,
                                                           preferred_element_type=F32))

    @pl.when(p == pages)
    def _():
        def causal(nrows):
            qrow = lax.broadcasted_iota(I32, (nrows, LANES), 0) % dec
            return lax.broadcasted_iota(I32, (nrows, LANES), 1) <= qrow

        for h in range(H_A):
            rows = slice(2 * dec * h, 2 * dec * (h + 1))
            kn = _pad_rows(_cols(proj_ref, (AK + h) * LANES, LANES).astype(BF16), LANES)
            vn = _pad_rows(_cols(proj_ref, (AV + h) * LANES, LANES).astype(BF16), LANES)
            s = lax.dot_general(qa_scr[rows, :], kn, NT, preferred_element_type=F32)
            s = jnp.where(causal(2 * dec), s, NEG)
            online(states[0], rows, s, lambda pr, vn=vn: jnp.dot(pr, vn, preferred_element_type=F32))
        for state, q_scr, kcol, vcol, bias, masked in ((states[1], qb_scr, BK, BV, b_bias, True),
                                                       (states[2], qc_scr, CK, CV, c_bias, False)):
            kn = _pad_rows(proj_ref[:, kcol * LANES:kcol * LANES + wide].astype(BF16), LANES)
            vn = _pad_rows(proj_ref[:, vcol * LANES:vcol * LANES + wide].astype(BF16), LANES)
            s = lax.dot_general(q_scr[...], kn, NT, preferred_element_type=F32) + bias()
            if masked:
                s = jnp.where(causal(H_B * dec), s, NEG)
            online(state, everything, s, lambda pr, vn=vn: jnp.dot(pr, vn, preferred_element_type=F32))

        lam, lam_init = _diff_lambda(lam_ref, layer)
        outs = []
        for h in range(H_A):
            r0 = slice(2 * dec * h, 2 * dec * h + dec)
            r1 = slice(2 * dec * h + dec, 2 * dec * (h + 1))
            outs.append(_diff_finish(acca[r0, :] / la[r0, :], acca[r1, :] / la[r1, :], lam, lam_init, g_ref[...]))
        oa_ref[...] = jnp.concatenate(outs, axis=1)
        for o_ref, l_ref, acc_ref in ((ob_ref, lb, accb), (oc_ref, lc, accc)):
            out = jnp.zeros((dec, wide), F32)
            for h in range(H_B):
                rows = slice(dec * h, dec * (h + 1))
                out = out + jnp.where(colhead == h, acc_ref[rows, :] / l_ref[rows, :], 0.0)
            o_ref[...] = out


def _sample_attn(pt_flat, p32s, caches, csum_t, cq, bias, lam, g, layer, nseq, dec, pages):
    width = (pages + 1) * LANES
    wide = H_B * DH

    def page_map(nd):
        def index_map(b, p, pt):
            return (layer, pt[b * pages + jnp.minimum(p, pages - 1)]) + (0,) * (nd - 2)
        return index_map

    seq_map = lambda b, p, pt: (b, 0, 0)
    in_specs = [pl.BlockSpec((dec, N_MAIN), lambda b, p, pt: (b, 0))]
    for c in caches:
        in_specs.append(pl.BlockSpec((1, 1) + c.shape[2:], page_map(c.ndim)))
    in_specs += [
        pl.BlockSpec((1, H_B, width), seq_map),
        pl.BlockSpec((1, dec, H_B), seq_map),
        pl.BlockSpec((1, dec, width), seq_map),
        pl.BlockSpec((4, DH), lambda b, p, pt: (0, 0)),
        pl.BlockSpec((1, 2 * DH), lambda b, p, pt: (0, 0)),
    ]
    rows = H_B * dec
    st = lambda w: [pltpu.VMEM((rows, 1), F32), pltpu.VMEM((rows, 1), F32), pltpu.VMEM((rows, w), F32)]
    grid_spec = pltpu.PrefetchScalarGridSpec(
        num_scalar_prefetch=1,
        grid=(nseq, pages + 1),
        in_specs=in_specs,
        out_specs=[pl.BlockSpec((dec, 4 * LANES), lambda b, p, pt: (b, 0))] * 3,
        scratch_shapes=[pltpu.VMEM((rows, LANES), BF16), pltpu.VMEM((rows, wide), BF16), pltpu.VMEM((rows, wide), BF16),
                        pltpu.VMEM((rows, 1), F32)] + st(2 * DH) + st(wide) + st(wide),
    )
    return pl.pallas_call(
        functools.partial(_sample_attn_kernel, pages=pages, layer=layer, dec=dec),
        grid_spec=grid_spec,
        out_shape=[jax.ShapeDtypeStruct((nseq * dec, 4 * LANES), F32)] * 3,
        compiler_params=_cparams(("parallel", "arbitrary")),
        name="sample_attention",
    )(pt_flat, p32s, *caches, csum_t, cq, bias, lam, g)


def _layer_norm(h, g, b):
    mu = jnp.mean(h, axis=-1, keepdims=True)
    var = jnp.mean(jnp.square(h - mu), axis=-1, keepdims=True)
    return (h - mu) * lax.rsqrt(var + LN_EPS) * g + b


def _merge_kernel(oa_ref, ob_ref, oc_ref, g_ref, x_ref, wa_ref, wb_ref, wc_ref, wo_ref, lg_ref, lb_ref,
                  wr_ref, br_ref, x1_ref, x16_ref, gate_ref, *, alpha):
    ya = jnp.dot(oa_ref[...], wa_ref[...], preferred_element_type=F32)
    yb = jnp.dot(ob_ref[...], wb_ref[...], preferred_element_type=F32)
    yc = jnp.dot(oc_ref[...], wc_ref[...], preferred_element_type=F32)
    merged = (g_ref[:, 0:D_MODEL] * ya + g_ref[:, D_MODEL:2 * D_MODEL] * yb
              + g_ref[:, 2 * D_MODEL:3 * D_MODEL] * yc)
    mix = jnp.dot(merged.astype(BF16), wo_ref[...], preferred_element_type=F32)
    x1 = _layer_norm(alpha * x_ref[...] + mix, lg_ref[...], lb_ref[...])
    x1_ref[...] = x1
    x16_ref[...] = x1.astype(BF16)
    logits = jnp.dot(x1, wr_ref[...], precision=lax.Precision.HIGHEST, preferred_element_type=F32) + br_ref[...]
    lane = lax.broadcasted_iota(I32, logits.shape, 1)
    work = logits
    vals, hots = [], []
    for _ in range(TOP_K_EXPERTS):
        mx = jnp.max(work, axis=1, keepdims=True)
        ix = jnp.min(jnp.where(work == mx, lane, N_EXPERTS), axis=1, keepdims=True)
        hot = lane == ix
        vals.append(mx)
        hots.append(hot)
        work = jnp.where(hot, -jnp.inf, work)
    es = [jnp.exp(v - vals[0]) for v in vals]
    den = es[0] + es[1] + es[2] + es[3]
    gate = jnp.zeros(logits.shape, F32)
    for e, hot in zip(es, hots):
        gate = gate + jnp.where(hot, e / den, 0.0)
    gate_ref[...] = gate


def _merge(oa, ob, oc, gates, x, wa, wb, wc, wo, lg, lb, wr, br, tm, alpha):
    m = x.shape[0]
    row = lambda w: pl.BlockSpec((tm, w), lambda i: (i, 0))
    full = lambda a: pl.BlockSpec(a.shape, lambda i: (0, 0))
    return pl.pallas_call(
        functools.partial(_merge_kernel, alpha=alpha),
        grid=(m // tm,),
        in_specs=[row(4 * LANES), row(4 * LANES), row(4 * LANES), row(3 * D_MODEL), row(D_MODEL),
                  full(wa), full(wb), full(wc), full(wo), full(lg), full(lb), full(wr), full(br)],
        out_specs=[row(D_MODEL), row(D_MODEL), row(N_EXPERTS)],
        out_shape=[jax.ShapeDtypeStruct((m, D_MODEL), F32), jax.ShapeDtypeStruct((m, D_MODEL), BF16),
                   jax.ShapeDtypeStruct((m, N_EXPERTS), F32)],
        compiler_params=_cparams(("parallel",)),
        name="merge_ln_router",
    )(oa, ob, oc, gates, x, wa, wb, wc, wo, lg, lb, wr, br)


MOE_CHUNK = 128


def _moe_kernel(cnt_ref, x16_ref, x1_ref, gate_ref, gatet_ref, w1g_ref, w1l_ref, b1g_ref, b1l_ref, w2_ref, b2_ref,
                lg_ref, lb_ref, o_ref, acc_scr, rank_scr, rankt_scr, *, alpha, tm):
    i = pl.program_id(0)
    e = pl.program_id(1)

    @pl.when(e == 0)
    def _():
        acc_scr[...] = jnp.zeros(acc_scr.shape, F32)
        r = lax.broadcasted_iota(I32, (tm, tm), 0)
        c = lax.broadcasted_iota(I32, (tm, tm), 1)
        sel = jnp.where(gate_ref[...] > 0.0, 1.0, 0.0).astype(BF16)
        selt = jnp.where(gatet_ref[...] > 0.0, 1.0, 0.0).astype(BF16)
        rank_scr[...] = jnp.dot((c < r).astype(BF16), sel, preferred_element_type=F32)
        rankt_scr[...] = jnp.dot(selt, (r < c).astype(BF16), preferred_element_type=F32)

    hot = lax.broadcasted_iota(I32, (1, N_EXPERTS), 1) == e
    g_col = jnp.sum(jnp.where(hot, gate_ref[...], 0.0), axis=1, keepdims=True)
    r_col = jnp.sum(jnp.where(hot, rank_scr[...], 0.0), axis=1, keepdims=True)
    g_row = gatet_ref[pl.ds(e, 1), :]
    r_row = rankt_scr[pl.ds(e, 1), :]
    slot_r = lax.broadcasted_iota(I32, (MOE_CHUNK, 1), 0).astype(F32)
    slot_c = lax.broadcasted_iota(I32, (1, MOE_CHUNK), 1).astype(F32)
    nch = (cnt_ref[i * N_EXPERTS + e] + MOE_CHUNK - 1) // MOE_CHUNK

    def chunk(ci, _):
        base = (ci * MOE_CHUNK).astype(F32)
        pick = ((r_row - base) == slot_r) & (g_row > 0.0)
        pick_t = ((r_col - base) == slot_c) & (g_col > 0.0)
        xc = jnp.dot(jnp.where(pick, 1.0, 0.0).astype(BF16), x16_ref[...], preferred_element_type=F32).astype(BF16)
        gw = jnp.sum(jnp.where(pick, g_row, 0.0), axis=1, keepdims=True)
        glu = jnp.minimum(jnp.dot(xc, w1g_ref[0], preferred_element_type=F32) + b1g_ref[0], SWIGLU_LIMIT)
        lin = jnp.clip(jnp.dot(xc, w1l_ref[0], preferred_element_type=F32) + b1l_ref[0], -SWIGLU_LIMIT, SWIGLU_LIMIT)
        a = glu * jax.nn.sigmoid(SWIGLU_ALPHA * glu) * (lin + 1.0)
        y = jnp.dot(a.astype(BF16), w2_ref[0], preferred_element_type=F32) + b2_ref[0]
        acc_scr[...] += jnp.dot(jnp.where(pick_t, 1.0, 0.0).astype(BF16), (y * gw).astype(BF16),
                                preferred_element_type=F32)
        return 0

    lax.fori_loop(0, nch, chunk, 0)

    @pl.when(e == N_EXPERTS - 1)
    def _():
        o_ref[...] = _layer_norm(alpha * x1_ref[...] + acc_scr[...], lg_ref[...], lb_ref[...])


def _moe(x16, x1, gate, w1p, b1g, b1l, w2, b2, lg, lb, tm, alpha):
    m = x1.shape[0]
    d_ff = w2.shape[1]
    nt = m // tm
    counts = jnp.sum((gate > 0.0).reshape(nt, tm, N_EXPERTS), axis=1).astype(I32).reshape(-1)
    gate_t = gate.T
    row = lambda w: pl.BlockSpec((tm, w), lambda i, e, c: (i, 0))
    grid_spec = pltpu.PrefetchScalarGridSpec(
        num_scalar_prefetch=1,
        grid=(nt, N_EXPERTS),
        in_specs=[row(D_MODEL), row(D_MODEL), row(N_EXPERTS),
                  pl.BlockSpec((N_EXPERTS, tm), lambda i, e, c: (0, i)),
                  pl.BlockSpec((1, D_MODEL, d_ff), lambda i, e, c: (e, 0, 0)),
                  pl.BlockSpec((1, D_MODEL, d_ff), lambda i, e, c: (e, 0, 1)),
                  pl.BlockSpec((1, 1, d_ff), lambda i, e, c: (e, 0, 0)),
                  pl.BlockSpec((1, 1, d_ff), lambda i, e, c: (e, 0, 0)),
                  pl.BlockSpec((1, d_ff, D_MODEL), lambda i, e, c: (e, 0, 0)),
                  pl.BlockSpec((1, 1, D_MODEL), lambda i, e, c: (e, 0, 0)),
                  pl.BlockSpec((1, D_MODEL), lambda i, e, c: (0, 0)),
                  pl.BlockSpec((1, D_MODEL), lambda i, e, c: (0, 0))],
        out_specs=row(D_MODEL),
        scratch_shapes=[pltpu.VMEM((tm, D_MODEL), F32), pltpu.VMEM((tm, N_EXPERTS), F32),
                        pltpu.VMEM((N_EXPERTS, tm), F32)],
    )
    return pl.pallas_call(
        functools.partial(_moe_kernel, alpha=alpha, tm=tm),
        grid_spec=grid_spec,
        out_shape=jax.ShapeDtypeStruct((m, D_MODEL), F32),
        compiler_params=_cparams(("parallel", "arbitrary")),
        name="moe_ln",
    )(counts, x16, x1, gate, gate_t, w1p, w1p, b1g, b1l, w2, b2, lg, lb)


def _deinterleave_w1(w1):
    f2 = w1.shape[-1]
    f = lax.broadcasted_iota(I32, (f2, f2), 0)
    g = lax.broadcasted_iota(I32, (f2, f2), 1)
    src = jnp.where(g < f2 // 2, 2 * g, 2 * (g - f2 // 2) + 1)
    perm = (f == src).astype(BF16)
    return jnp.einsum("edf,fg->edg", w1.astype(BF16), perm, preferred_element_type=BF16)


def _repack_w_in(w, b):
    offs = np.concatenate([[0], np.cumsum(PROJ_SIZES)])
    seg = lambda a, i: a[..., int(offs[i]):int(offs[i + 1])]
    d = w.shape[0]

    def build(a, rows):
        z = lambda n: jnp.zeros(rows + (n,), a.dtype)
        misc = jnp.concatenate([z(MISC_BF), seg(a, 6), seg(a, 12), z(LANES - MISC_IW - H_I)], axis=-1)
        main = jnp.concatenate([seg(a, 0), seg(a, 1), seg(a, 2), seg(a, 3), seg(a, 4), seg(a, 5),
                                seg(a, 7), seg(a, 8), seg(a, 9), seg(a, 10), seg(a, 11), seg(a, 11), misc], axis=-1)
        return main, seg(a, 13)

    wm, wg = build(w, (d,))
    bm, bg = build(b[None, :], (1,))
    return wm.astype(BF16), bm, wg.astype(BF16), bg


def _rope_tables(pos):
    half = ROT_DIM // 2
    inv = ROPE_THETA ** (-jnp.arange(half, dtype=F32) / half)
    ang = pos.astype(F32)[:, None] * inv
    cos, sin = jnp.cos(ang), jnp.sin(ang)
    m = pos.shape[0]
    ones = jnp.ones((m, DH - ROT_DIM), F32)
    zeros = jnp.zeros((m, DH - ROT_DIM), F32)
    z8 = jnp.zeros((m, half), F32)
    c64 = jnp.concatenate([cos, cos, ones], axis=1)
    a64 = jnp.concatenate([-sin, z8, zeros], axis=1)
    b64 = jnp.concatenate([z8, sin, zeros], axis=1)
    dup = lambda a: jnp.concatenate([a, a], axis=1)
    return dup(c64), dup(a64), dup(b64)


def kernel(x_prompt, x_sample, cache_a_k, cache_a_v, cache_b_k, cache_b_v, cache_b_logf, cache_c_k, cache_c_v,
           cache_c_idx_k, page_table, meta, w_in, b_in, a_lambda, a_norm_g, w_br_a, w_br_b, w_br_c, w_out,
           ln1_g, ln1_b, ln2_g, ln2_b, w_router, b_router, w_mlp1, b_mlp1, w_mlp2, b_mlp2):
    n_p, seq, _ = x_prompt.shape
    nseq, dec, _ = x_sample.shape
    depth = w_in.shape[0]
    pages = page_table.shape[1]
    past_len = pages * PAGE
    t = seq + BLOCK_Q
    topk_prompt = min(TOPK_MAX, seq // 4)
    topk_sample = min(TOPK_MAX, (past_len + dec) // 4)
    alpha = (2 * depth) ** 0.25
    mp = n_p * t
    m_all = mp + nseq * dec
    tm = _pick(m_all, (768, 640, 512, 384, 256, 128, 64, 32, 16, 8))

    xp = jnp.concatenate([jnp.zeros((n_p, META_PAD, D_MODEL), F32),
                          jnp.broadcast_to(meta[None], (n_p, N_META, D_MODEL)), x_prompt], axis=1)
    x = jnp.concatenate([xp.reshape(mp, D_MODEL), x_sample.reshape(nseq * dec, D_MODEL)], axis=0)
    pos = jnp.concatenate([jnp.tile(jnp.arange(t, dtype=I32) - META_PAD, n_p),
                           jnp.tile(past_len + jnp.arange(dec, dtype=I32), nseq)])
    cos, sa, sb = _rope_tables(pos)
    pt_flat = page_table.reshape(-1).astype(I32)
    flat_a = lambda c: c.reshape(c.shape[0], c.shape[1], PAGE * H_A, 2 * DH)
    key_minor = lambda c: jnp.transpose(c, (0, 1, 3, 4, 2))
    caches = (flat_a(cache_a_k), flat_a(cache_a_v), key_minor(cache_b_k), key_minor(cache_b_v),
              key_minor(cache_c_k), key_minor(cache_c_v))
    idx_kt = jnp.transpose(cache_c_idx_k, (0, 1, 3, 2))
    logf_t_cache = jnp.transpose(cache_b_logf, (0, 1, 3, 2))

    rows_p, rows_s = [], []
    for layer in range(depth):
        wm, bm, wg, bg = _repack_w_in(w_in[layer], b_in[layer])
        x16 = x.astype(BF16)
        p32, p16 = _project_main(x16, wm, bm, cos, sa, sb, tm)
        gates = _project_gates(x16, wg, bg, tm)
        lam = a_lambda[layer]
        g = a_norm_g[layer][None, :]

        bf_t = jnp.swapaxes(p32[:mp, MISC * LANES + MISC_BF:MISC * LANES + MISC_BF + H_B].reshape(n_p, t, H_B), 1, 2)
        logf_t, csum_t = _prompt_cumsum(bf_t)
        bias = _prompt_index(p16, p32, n_p, t, topk_prompt)
        oa = _prompt_flash("A", p16, (lam, g), n_p, t, layer)
        ob = _prompt_flash("B", p16, (jnp.swapaxes(csum_t, 1, 2), csum_t), n_p, t, layer)
        oc = _prompt_flash("C", p16, (bias,), n_p, t, layer)

        p32s = p32[mp:]
        sbias, scsum, scq, slogf = _sample_pre(pt_flat, p32s, idx_kt, logf_t_cache, layer, nseq, dec,
                                               pages, topk_sample)
        soa, sob, soc = _sample_attn(pt_flat, p32s, caches, scsum, scq, sbias, lam, g, layer, nseq, dec, pages)

        oa = jnp.concatenate([oa, soa.astype(BF16)], axis=0)
        ob = jnp.concatenate([ob, sob.astype(BF16)], axis=0)
        oc = jnp.concatenate([oc, soc.astype(BF16)], axis=0)
        x1, x1_16, gate = _merge(oa, ob, oc, gates, x,
                                 w_br_a[layer].astype(BF16), w_br_b[layer].astype(BF16), w_br_c[layer].astype(BF16),
                                 w_out[layer].astype(BF16), ln1_g[layer][None], ln1_b[layer][None],
                                 w_router[layer], b_router[layer][None], tm, alpha)
        x = _moe(x1_16, x1, gate, _deinterleave_w1(w_mlp1[layer]),
                 b_mlp1[layer][:, None, 0::2], b_mlp1[layer][:, None, 1::2],
                 w_mlp2[layer].astype(BF16), b_mlp2[layer][:, None, :],
                 ln2_g[layer][None], ln2_b[layer][None], tm, alpha)

        def seg(rows, col, width, shape):
            return rows[:, col * LANES:col * LANES + width].reshape(shape)

        pp = p32[:mp].reshape(n_p, t, N_MAIN)[:, META_PAD:].reshape(n_p * (t - META_PAD), N_MAIN)
        tp = t - META_PAD
        rows_p.append((seg(pp, AK, W_A, (n_p, tp, H_A, 2 * DH)), seg(pp, AV, W_A, (n_p, tp, H_A, 2 * DH)),
                       seg(pp, BK, W_B, (n_p, tp, H_B, DH)), seg(pp, BV, W_B, (n_p, tp, H_B, DH)),
                       jnp.swapaxes(logf_t, 1, 2)[:, META_PAD:],
                       seg(pp, CK, W_C, (n_p, tp, H_C, DH)), seg(pp, CV, W_C, (n_p, tp, H_C, DH)),
                       seg(pp, IK, D_IDX, (n_p, tp, D_IDX))))
        rows_s.append((seg(p32s, AK, W_A, (nseq, dec, H_A, 2 * DH)), seg(p32s, AV, W_A, (nseq, dec, H_A, 2 * DH)),
                       seg(p32s, BK, W_B, (nseq, dec, H_B, DH)), seg(p32s, BV, W_B, (nseq, dec, H_B, DH)),
                       slogf,
                       seg(p32s, CK, W_C, (nseq, dec, H_C, DH)), seg(p32s, CV, W_C, (nseq, dec, H_C, DH)),
                       seg(p32s, IK, D_IDX, (nseq, dec, D_IDX))))

    st = lambda rows, i: jnp.stack([r[i] for r in rows])
    y_prompt = x[:mp].reshape(n_p, t, D_MODEL)[:, BLOCK_Q:]
    y_sample = x[mp:].reshape(nseq, dec, D_MODEL)
    return (y_prompt, y_sample) + tuple(st(rows_p, i) for i in range(8)) + tuple(st(rows_s, i) for i in range(8))
```

```python
import functools
import math

import numpy as np
import jax
import jax.numpy as jnp
from jax import lax
from jax.experimental import pallas as pl
from jax.experimental.pallas import tpu as pltpu

F32 = jnp.float32
BF16 = jnp.bfloat16
I32 = jnp.int32

D_MODEL = 1024
DH = 64
H_A = 4
H_B = 8
H_C = 8
H_I = 4
D_IDX = 64
TOPK_MAX = 256
N_META = 16
BLOCK_Q = 128
META_PAD = BLOCK_Q - N_META
ROT_DIM = DH // 4
ROPE_THETA = 500000.0
N_EXPERTS = 32
TOP_K_EXPERTS = 4
SWIGLU_ALPHA = 1.702
SWIGLU_LIMIT = 7.0
LN_EPS = 1e-5
RMS_EPS = 1e-5
NEG = -1e30
M_INIT = float(np.finfo(np.float32).min)
PAGE = 128
LANES = 128
W_A = H_A * 2 * DH
W_B = H_B * DH
W_C = H_C * DH
PROJ_SIZES = (W_A, W_A, W_A, W_B, W_B, W_B, H_B, W_C, W_C, W_C, H_I * D_IDX, D_IDX, H_I, 3 * D_MODEL)
SCALE = DH ** -0.5
IW_SCALE = H_I ** -0.5 * D_IDX ** -0.5

AQ, AK, AV, BQ, BK, BV, CQ, CK, CV, IQ, IK, MISC = 0, 4, 8, 12, 16, 20, 24, 28, 32, 36, 38, 39
N_MAIN = 40 * LANES
MISC_BF = 16
MISC_IW = 24
PROJ_TN = 512
ROPE_TILES = (1, 1, 0, 0, 0, 0, 1, 1, 0, 1)
VMEM_LIMIT = 56 * 1024 * 1024
INT_MIN = -2 ** 31

NT = (((1,), (1,)), ((), ()))


def _pick(n, cands):
    for c in cands:
        if n % c == 0:
            return c
    raise ValueError(f"no tile for {n}")


def _cparams(sem):
    return pltpu.CompilerParams(dimension_semantics=sem, vmem_limit_bytes=VMEM_LIMIT)


def _log_sigmoid(x):
    return jnp.minimum(x, 0.0) - jnp.log1p(jnp.exp(-jnp.abs(x)))


def _lane_scan(x):
    lane = lax.broadcasted_iota(I32, x.shape, 1)
    s = 1
    while s < LANES:
        x = x + jnp.where(lane >= s, pltpu.roll(x, s, 1), 0.0)
        s *= 2
    return x


def _rope_tile(x, cos, sa, sb):
    outs = []
    for c in range(x.shape[1] // LANES):
        xc = x[:, c * LANES:(c + 1) * LANES]
        up = pltpu.roll(xc, LANES - ROT_DIM // 2, 1)
        dn = pltpu.roll(xc, ROT_DIM // 2, 1)
        outs.append(xc * cos + up * sa + dn * sb)
    return jnp.concatenate(outs, axis=1)


def _proj_kernel(flags_ref, x_ref, w_ref, b_ref, cos_ref, sa_ref, sb_ref, o32_ref, o16_ref):
    j = pl.program_id(1)
    acc = jnp.dot(x_ref[...], w_ref[...], preferred_element_type=F32) + b_ref[...]

    @pl.when(flags_ref[j] == 0)
    def _():
        o32_ref[...] = acc
        o16_ref[...] = acc.astype(BF16)

    @pl.when(flags_ref[j] == 1)
    def _():
        r = _rope_tile(acc, cos_ref[...], sa_ref[...], sb_ref[...])
        o32_ref[...] = r
        o16_ref[...] = r.astype(BF16)


def _project_main(x16, w, b, cos, sa, sb, tm):
    m = x16.shape[0]
    n = w.shape[1]
    flags = jnp.asarray(ROPE_TILES, I32)
    grid_spec = pltpu.PrefetchScalarGridSpec(
        num_scalar_prefetch=1,
        grid=(m // tm, n // PROJ_TN),
        in_specs=[
            pl.BlockSpec((tm, D_MODEL), lambda i, j, f: (i, 0)),
            pl.BlockSpec((D_MODEL, PROJ_TN), lambda i, j, f: (0, j)),
            pl.BlockSpec((1, PROJ_TN), lambda i, j, f: (0, j)),
            pl.BlockSpec((tm, LANES), lambda i, j, f: (i, 0)),
            pl.BlockSpec((tm, LANES), lambda i, j, f: (i, 0)),
            pl.BlockSpec((tm, LANES), lambda i, j, f: (i, 0)),
        ],
        out_specs=[
            pl.BlockSpec((tm, PROJ_TN), lambda i, j, f: (i, j)),
            pl.BlockSpec((tm, PROJ_TN), lambda i, j, f: (i, j)),
        ],
    )
    return pl.pallas_call(
        _proj_kernel,
        grid_spec=grid_spec,
        out_shape=[jax.ShapeDtypeStruct((m, n), F32), jax.ShapeDtypeStruct((m, n), BF16)],
        compiler_params=_cparams(("parallel", "arbitrary")),
        name="proj_main",
    )(flags, x16, w, b, cos, sa, sb)


def _gate_kernel(x_ref, w_ref, b_ref, o_ref):
    acc = jnp.dot(x_ref[...], w_ref[...], preferred_element_type=F32) + b_ref[...]
    o_ref[...] = jax.nn.sigmoid(acc)


def _project_gates(x16, w, b, tm):
    m = x16.shape[0]
    n = w.shape[1]
    return pl.pallas_call(
        _gate_kernel,
        grid=(m // tm, n // PROJ_TN),
        in_specs=[
            pl.BlockSpec((tm, D_MODEL), lambda i, j: (i, 0)),
            pl.BlockSpec((D_MODEL, PROJ_TN), lambda i, j: (0, j)),
            pl.BlockSpec((1, PROJ_TN), lambda i, j: (0, j)),
        ],
        out_specs=pl.BlockSpec((tm, PROJ_TN), lambda i, j: (i, j)),
        out_shape=jax.ShapeDtypeStruct((m, n), F32),
        compiler_params=_cparams(("parallel", "arbitrary")),
        name="proj_gates",
    )(x16, w, b)


def _cumsum_kernel(bf_ref, logf_ref, csum_ref, *, t):
    def body(c, carry):
        sl = pl.ds(pl.multiple_of(c * LANES, LANES), LANES)
        tok = c * LANES + lax.broadcasted_iota(I32, (H_B, LANES), 1)
        lf = jnp.where(tok >= META_PAD, _log_sigmoid(bf_ref[0, :, sl]), 0.0)
        logf_ref[0, :, sl] = lf
        cs = _lane_scan(lf) + carry
        csum_ref[0, :, sl] = cs
        return cs[:, LANES - 1:LANES]

    lax.fori_loop(0, t // LANES, body, jnp.zeros((H_B, 1), F32))


def _prompt_cumsum(bf_t):
    n, h, t = bf_t.shape
    return pl.pallas_call(
        functools.partial(_cumsum_kernel, t=t),
        grid=(n,),
        in_specs=[pl.BlockSpec((1, h, t), lambda b: (b, 0, 0))],
        out_specs=[pl.BlockSpec((1, h, t), lambda b: (b, 0, 0)),
                   pl.BlockSpec((1, h, t), lambda b: (b, 0, 0))],
        out_shape=[jax.ShapeDtypeStruct((n, h, t), F32), jax.ShapeDtypeStruct((n, h, t), F32)],
        compiler_params=_cparams(("parallel",)),
        name="prompt_cumsum",
    )(bf_t)


def _topk_select(sc_ref, key_ref, out_ref, nsuper, cw, k, rows, out_dtype):
    def sl_of(c):
        return pl.ds(pl.multiple_of(c * LANES, LANES), LANES)

    def make_keys(sc, _):
        for jj in range(cw):
            sl = sl_of(sc * cw + jj)
            bits = pltpu.bitcast(sc_ref[:, sl], I32)
            key_ref[:, sl] = jnp.where(bits < 0, bits ^ jnp.int32(0x7FFFFFFF), bits)
        return 0

    lax.fori_loop(0, nsuper, make_keys, 0)

    def count(pred):
        def body(sc, acc):
            for jj in range(cw):
                acc = acc + jnp.where(pred(key_ref[:, sl_of(sc * cw + jj)]), 1, 0).astype(I32)
            return acc
        acc = lax.fori_loop(0, nsuper, body, jnp.zeros((rows, LANES), I32))
        return jnp.sum(acc, axis=1, keepdims=True)

    c0 = count(lambda key: key >= 0)
    t0 = jnp.where(c0 >= k, jnp.int32(0), jnp.int32(INT_MIN))

    def bit_body(i, t):
        cand = t | jnp.left_shift(jnp.int32(1), 30 - i)
        cnt = count(lambda key: key >= cand)
        return jnp.where(cnt >= k, cand, t)

    thr = lax.fori_loop(0, 31, bit_body, t0)
    need = (k - count(lambda key: key > thr)).astype(F32)
    upper = (lax.broadcasted_iota(I32, (LANES, LANES), 0)
             < lax.broadcasted_iota(I32, (LANES, LANES), 1)).astype(BF16)

    def sel_body(sc, carry):
        for jj in range(cw):
            sl = sl_of(sc * cw + jj)
            key = key_ref[:, sl]
            eq = key == thr
            eqf = jnp.where(eq, 1.0, 0.0)
            rank = jnp.dot(eqf.astype(BF16), upper, preferred_element_type=F32) + carry
            sel = (key > thr) | (eq & (rank < need))
            valid = sc_ref[:, sl] > 0.5 * NEG
            out_ref[:, sl] = jnp.where(sel & valid, 0.0, NEG).astype(out_dtype)
            carry = carry + jnp.sum(eqf, axis=1, keepdims=True)
        return carry

    lax.fori_loop(0, nsuper, sel_body, jnp.zeros((rows, 1), F32))


def _prompt_index_kernel(iq_ref, ik_ref, misc_ref, bias_ref, sc_scr, key_scr, *, k, cw):
    i = pl.program_id(1)
    wide = cw * LANES
    nsuper = (i + cw) // cw
    lo = lax.broadcasted_iota(I32, (1, LANES), 1) < DH
    zero = jnp.zeros((BLOCK_Q, LANES), BF16)
    iq_heads = []
    for pair in range(H_I // 2):
        blk = iq_ref[:, pair * LANES:(pair + 1) * LANES]
        iq_heads.append(jnp.where(lo, blk, zero))
        iq_heads.append(jnp.where(lo, zero, blk))
    iw = misc_ref[...][:, MISC_IW:MISC_IW + H_I] * IW_SCALE
    iw_wide = [jnp.broadcast_to(iw[:, h:h + 1], (BLOCK_Q, wide)) for h in range(H_I)]
    qidx = i * BLOCK_Q + lax.broadcasted_iota(I32, (BLOCK_Q, 1), 0)

    def chunk(c, _):
        sl = pl.ds(pl.multiple_of(c * wide, wide), wide)
        kmat = ik_ref[sl, :]
        acc = None
        for h in range(H_I):
            s = lax.dot_general(iq_heads[h], kmat, NT, preferred_element_type=F32)
            term = jnp.maximum(s, 0.0) * iw_wide[h]
            acc = term if acc is None else acc + term
        kidx = c * wide + lax.broadcasted_iota(I32, (1, wide), 1)
        ok = (kidx <= qidx) & (kidx >= META_PAD)
        sc_scr[:, sl] = jnp.where(ok, acc, NEG)
        return 0

    lax.fori_loop(0, nsuper, chunk, 0)
    bias_ref[...] = jnp.full(bias_ref.shape, NEG, bias_ref.dtype)
    _topk_select(sc_scr, key_scr, bias_ref.at[0], nsuper, cw, k, BLOCK_Q, bias_ref.dtype)


def _prompt_index(p16, p32, n, t, k):
    nq = t // BLOCK_Q
    return pl.pallas_call(
        functools.partial(_prompt_index_kernel, k=k, cw=_pick(t, (640, 512, 384, 256, 128)) // LANES),
        grid=(n, nq),
        in_specs=[
            pl.BlockSpec((BLOCK_Q, 2 * LANES), lambda b, i: (b * nq + i, IQ // 2)),
            pl.BlockSpec((t, LANES), lambda b, i: (b, IK)),
            pl.BlockSpec((BLOCK_Q, LANES), lambda b, i: (b * nq + i, MISC)),
        ],
        out_specs=pl.BlockSpec((1, BLOCK_Q, t), lambda b, i: (b, i, 0)),
        out_shape=jax.ShapeDtypeStruct((n, t, t), BF16),
        scratch_shapes=[pltpu.VMEM((BLOCK_Q, t), F32), pltpu.VMEM((BLOCK_Q, t), I32)],
        compiler_params=_cparams(("parallel", "arbitrary")),
        name="prompt_index_topk",
    )(p16, p16, p32)


def _diff_lambda(lam_ref, layer):
    lam_init = 0.8 - 0.6 * math.exp(-0.3 * layer)
    lv = lam_ref[...]
    a = jnp.sum(lv[0:1] * lv[1:2], axis=1, keepdims=True)
    b = jnp.sum(lv[2:3] * lv[3:4], axis=1, keepdims=True)
    return jnp.exp(a) - jnp.exp(b) + lam_init, lam_init


def _diff_finish(o0, o1, lam, lam_init, g):
    o = o0 - lam * o1
    o = o * lax.rsqrt(jnp.mean(jnp.square(o), axis=-1, keepdims=True) + RMS_EPS)
    return o * g * (1.0 - lam_init)


def _flash_kernel(qi_ref, kj_ref, *refs, mode, tq, tk, layer):
    if mode == "A":
        q_ref, k_ref, v_ref, lam_ref, g_ref, o_ref, m_scr, acc_scr = refs
    elif mode == "B":
        q_ref, k_ref, v_ref, cq_ref, ck_ref, o_ref, m_scr, acc_scr, cq_scr = refs
    else:
        q_ref, k_ref, v_ref, bias_ref, o_ref, m_scr, acc_scr = refs
    cb = pl.program_id(1)
    step = pl.program_id(2)
    i = qi_ref[step]
    j = kj_ref[step]
    lo = lax.broadcasted_iota(I32, (1, LANES), 1) < DH

    @pl.when(j == 0)
    def _():
        m_scr[...] = jnp.full(m_scr.shape, M_INIT, F32)
        acc_scr[...] = jnp.zeros(acc_scr.shape, F32)
        if mode == "B":
            lane8 = lax.broadcasted_iota(I32, (1, H_B), 1)
            for u in range(2):
                cq_scr[u] = jnp.sum(jnp.where(lane8 == 2 * cb + u, cq_ref[0], 0.0), axis=1, keepdims=True)

    def update(masked):
        q = q_ref[...] * jnp.asarray(SCALE, BF16)
        zero = jnp.zeros_like(q)
        k = k_ref[...]
        v_ext = jnp.concatenate([v_ref[...], jnp.ones((tk, LANES), BF16)], axis=1)
        if masked:
            qidx = i * tq + lax.broadcasted_iota(I32, (tq, 1), 0)
            kidx = j * tk + lax.broadcasted_iota(I32, (1, tk), 1)
            ok = (kidx <= qidx) & (kidx >= META_PAD)
        for u in range(2):
            qu = jnp.where(lo, q, zero) if u == 0 else jnp.where(lo, zero, q)
            s = lax.dot_general(qu, k, NT, preferred_element_type=F32)
            if mode == "B":
                s = s + cq_scr[u] - ck_ref[0, pl.ds(2 * cb + u, 1), :]
            if mode == "C":
                s = s + bias_ref[0].astype(F32)
            if masked:
                s = jnp.where(ok, s, NEG)
            m_prev = m_scr[u]
            m_new = jnp.maximum(m_prev, jnp.max(s, axis=1, keepdims=True))
            alpha = jnp.exp(m_prev - m_new)
            p = jnp.exp((s - m_new).astype(BF16))
            acc_scr[u] = alpha * acc_scr[u] + jnp.dot(p, v_ext, preferred_element_type=F32)
            m_scr[u] = m_new

    if mode == "C":
        update(False)
    else:
        edge = (j == i) | (j == 0)

        @pl.when(edge)
        def _():
            update(True)

        @pl.when(jnp.logical_not(edge))
        def _():
            update(False)

    @pl.when(j == i)
    def _():
        o0 = acc_scr[0, :, 0:LANES] / acc_scr[0, :, LANES:LANES + 1]
        o1 = acc_scr[1, :, 0:LANES] / acc_scr[1, :, LANES:LANES + 1]
        if mode == "A":
            lam, lam_init = _diff_lambda(lam_ref, layer)
            o_ref[...] = _diff_finish(o0, o1, lam, lam_init, g_ref[...]).astype(o_ref.dtype)
        else:
            o_ref[...] = jnp.where(lo, o0, o1).astype(o_ref.dtype)


def _prompt_flash(mode, p16, extras, n, t, layer):
    tq = _pick(t, (640, 512, 384, 256, 128))
    tk = tq
    nq = t // tq
    qi = np.array([i for i in range(nq) for _ in range(i + 1)], np.int32)
    kj = np.array([j for i in range(nq) for j in range(i + 1)], np.int32)
    qcol, kcol, vcol = {"A": (AQ, AK, AV), "B": (BQ, BK, BV), "C": (CQ, CK, CV)}[mode]
    in_specs = [
        pl.BlockSpec((tq, LANES), lambda b, c, s, qi, kj: (b * nq + qi[s], qcol + c)),
        pl.BlockSpec((tk, LANES), lambda b, c, s, qi, kj: (b * nq + kj[s], kcol + c)),
        pl.BlockSpec((tk, LANES), lambda b, c, s, qi, kj: (b * nq + kj[s], vcol + c)),
    ]
    scratch = [pltpu.VMEM((2, tq, 1), F32), pltpu.VMEM((2, tq, 2 * LANES), F32)]
    if mode == "A":
        in_specs += [pl.BlockSpec((4, DH), lambda b, c, s, qi, kj: (0, 0)),
                     pl.BlockSpec((1, 2 * DH), lambda b, c, s, qi, kj: (0, 0))]
    elif mode == "B":
        in_specs += [pl.BlockSpec((1, tq, H_B), lambda b, c, s, qi, kj: (b, qi[s], 0)),
                     pl.BlockSpec((1, H_B, tk), lambda b, c, s, qi, kj: (b, 0, kj[s]))]
        scratch += [pltpu.VMEM((2, tq, 1), F32)]
    else:
        in_specs += [pl.BlockSpec((1, tq, tk), lambda b, c, s, qi, kj: (b, qi[s], kj[s]))]
    grid_spec = pltpu.PrefetchScalarGridSpec(
        num_scalar_prefetch=2,
        grid=(n, 4, len(qi)),
        in_specs=in_specs,
        out_specs=pl.BlockSpec((tq, LANES), lambda b, c, s, qi, kj: (b * nq + qi[s], c)),
        scratch_shapes=scratch,
    )
    return pl.pallas_call(
        functools.partial(_flash_kernel, mode=mode, tq=tq, tk=tk, layer=layer),
        grid_spec=grid_spec,
        out_shape=jax.ShapeDtypeStruct((n * t, 4 * LANES), BF16),
        compiler_params=_cparams(("parallel", "parallel", "arbitrary")),
        name=f"prompt_flash_{mode}",
    )(jnp.asarray(qi), jnp.asarray(kj), p16, p16, p16, *extras)


def _pad_rows(x, rows):
    return jnp.concatenate([x, jnp.zeros((rows - x.shape[0], x.shape[1]), x.dtype)], axis=0)


def _cols(ref, start, width):
    blk = start // LANES
    off = start - blk * LANES
    x = ref[:, blk * LANES:(blk + 1) * LANES]
    return x if width == LANES else x[:, off:off + width]


def _stack_rows(x, reps):
    return jnp.concatenate([x] * reps, axis=0)


def _spread_rows(x, reps):
    return jnp.concatenate([jnp.broadcast_to(x[h:h + 1], (reps, x.shape[1])) for h in range(x.shape[0])], axis=0)


def _sample_pre_kernel(pt_ref, proj_ref, *refs, pages, group, k, dec):
    ikt_refs, lft_refs = refs[:group], refs[group:2 * group]
    bias_ref, csum_ref, cq_ref, lnew_ref, sc_scr, key_scr, carry_scr = refs[2 * group:]
    p = pl.program_id(1)
    nsteps = pages // group
    span = group * LANES
    iq = jnp.concatenate([_cols(proj_ref, IQ * LANES + h * D_IDX, D_IDX) for h in range(H_I)], axis=0).astype(BF16)
    iw = _cols(proj_ref, MISC * LANES + MISC_IW, H_I) * IW_SCALE
    iw_rows = jnp.concatenate([iw[:, h:h + 1] for h in range(H_I)], axis=0)

    def scores(s):
        w = jnp.maximum(s, 0.0) * iw_rows
        out = w[0:dec]
        for h in range(1, H_I):
            out = out + w[h * dec:(h + 1) * dec]
        return out

    @pl.when(p == 0)
    def _():
        carry_scr[...] = jnp.zeros(carry_scr.shape, F32)

    @pl.when(p < nsteps)
    def _():
        kt = jnp.concatenate([r[0, 0] for r in ikt_refs], axis=1).astype(BF16)
        sc_scr[:, pl.ds(pl.multiple_of(p * span, span), span)] = scores(
            jnp.dot(iq, kt, preferred_element_type=F32))
        scans = [_lane_scan(r[0, 0]) for r in lft_refs]
        carry = carry_scr[...]
        for g, sc in enumerate(scans):
            csum_ref[0, :, pl.ds(pl.multiple_of(p * span + g * LANES, LANES), LANES)] = sc + carry
            carry = carry + sc[:, LANES - 1:LANES]
        carry_scr[...] = carry

    @pl.when(p == nsteps)
    def _():
        tail = slice(pages * LANES, (pages + 1) * LANES)
        knew = _pad_rows(_cols(proj_ref, IK * LANES, D_IDX).astype(BF16), LANES)
        s = scores(lax.dot_general(iq, knew, NT, preferred_element_type=F32))
        row = lax.broadcasted_iota(I32, (dec, LANES), 0)
        lane = lax.broadcasted_iota(I32, (dec, LANES), 1)
        sc_scr[:, tail] = jnp.where(lane <= row, s, NEG)
        lnew = _log_sigmoid(_cols(proj_ref, MISC * LANES + MISC_BF, H_B))
        lnew_ref[0] = lnew
        eye = (lax.broadcasted_iota(I32, (H_B, H_B), 0) == lax.broadcasted_iota(I32, (H_B, H_B), 1)).astype(F32)
        lt = lax.dot_general(eye, _pad_rows(lnew, LANES), NT, precision=lax.Precision.HIGHEST,
                             preferred_element_type=F32)
        cs = _lane_scan(lt) + carry_scr[...]
        csum_ref[0, :, tail] = cs
        pick = (row == lane).astype(F32)
        cq_ref[0] = lax.dot_general(pick, cs, NT, precision=lax.Precision.HIGHEST, preferred_element_type=F32)
        _topk_select(sc_scr, key_scr, bias_ref.at[0], 1, pages + 1, k, dec, F32)


def _page_map(layer, pages, group, g, ndim):
    nsteps = pages // group

    def index_map(b, p, pt):
        return (layer, pt[b * pages + jnp.minimum(p, nsteps - 1) * group + g]) + (0,) * (ndim - 2)
    return index_map


def _sample_pre(pt_flat, p32s, ikt, lft, layer, nseq, dec, pages, k):
    width = (pages + 1) * LANES
    group = _pick(pages, (8, 4, 2, 1))
    in_specs = [pl.BlockSpec((dec, N_MAIN), lambda b, p, pt: (b, 0))]
    in_specs += [pl.BlockSpec((1, 1, D_IDX, PAGE), _page_map(layer, pages, group, g, 4)) for g in range(group)]
    in_specs += [pl.BlockSpec((1, 1, H_B, PAGE), _page_map(layer, pages, group, g, 4)) for g in range(group)]
    grid_spec = pltpu.PrefetchScalarGridSpec(
        num_scalar_prefetch=1,
        grid=(nseq, pages // group + 1),
        in_specs=in_specs,
        out_specs=[
            pl.BlockSpec((1, dec, width), lambda b, p, pt: (b, 0, 0)),
            pl.BlockSpec((1, H_B, width), lambda b, p, pt: (b, 0, 0)),
            pl.BlockSpec((1, dec, H_B), lambda b, p, pt: (b, 0, 0)),
            pl.BlockSpec((1, dec, H_B), lambda b, p, pt: (b, 0, 0)),
        ],
        scratch_shapes=[pltpu.VMEM((dec, width), F32), pltpu.VMEM((dec, width), I32), pltpu.VMEM((H_B, 1), F32)],
    )
    return pl.pallas_call(
        functools.partial(_sample_pre_kernel, pages=pages, group=group, k=k, dec=dec),
        grid_spec=grid_spec,
        out_shape=[jax.ShapeDtypeStruct((nseq, dec, width), F32),
                   jax.ShapeDtypeStruct((nseq, H_B, width), F32),
                   jax.ShapeDtypeStruct((nseq, dec, H_B), F32),
                   jax.ShapeDtypeStruct((nseq, dec, H_B), F32)],
        compiler_params=_cparams(("parallel", "arbitrary")),
        name="sample_index_topk",
    )(pt_flat, p32s, *([ikt] * group), *([lft] * group))


def _sample_attn_kernel(pt_ref, proj_ref, *refs, pages, group, layer, dec):
    caches = [refs[c * group:(c + 1) * group] for c in range(6)]
    ka_refs, va_refs, kb_refs, vb_refs, kc_refs, vc_refs = caches
    (csum_ref, cq_ref, bias_ref, lam_ref, g_ref, oa_ref, ob_ref, oc_ref,
     qa_scr, qb_scr, qc_scr, cqb_scr, ma, la, acca, mb, lb, accb, mc, lc, accc) = refs[6 * group:]
    nsteps = pages // group
    span = group * LANES
    p = pl.program_id(1)
    wide = H_B * DH
    lo = lax.broadcasted_iota(I32, (1, LANES), 1) < DH
    colhead = lax.broadcasted_iota(I32, (1, wide), 1) // DH
    states = ((ma, la, acca), (mb, lb, accb), (mc, lc, accc))

    @pl.when(p == 0)
    def _():
        for m_ref, l_ref, acc_ref in states:
            m_ref[...] = jnp.full(m_ref.shape, M_INIT, F32)
            l_ref[...] = jnp.zeros(l_ref.shape, F32)
            acc_ref[...] = jnp.zeros(acc_ref.shape, F32)
        pieces = []
        for h in range(H_A):
            qh = _cols(proj_ref, (AQ + h) * LANES, LANES) * SCALE
            pieces += [jnp.where(lo, qh, 0.0), jnp.where(lo, 0.0, qh)]
        qa_scr[...] = jnp.concatenate(pieces, axis=0).astype(BF16)
        for q_scr, col in ((qb_scr, BQ), (qc_scr, CQ)):
            q_all = proj_ref[:, col * LANES:col * LANES + wide] * SCALE
            q_scr[...] = jnp.concatenate([jnp.where(colhead == h, q_all, 0.0) for h in range(H_B)],
                                         axis=0).astype(BF16)
        cq = cq_ref[0]
        cqb_scr[...] = jnp.concatenate([cq[:, h:h + 1] for h in range(H_B)], axis=0)

    def online(state, rows, s, pv):
        m_ref, l_ref, acc_ref = state
        m_prev = m_ref[rows, :]
        m_new = jnp.maximum(m_prev, jnp.max(s, axis=1, keepdims=True))
        alpha = jnp.exp(m_prev - m_new)
        pr = jnp.exp(s - m_new)
        l_ref[rows, :] = alpha * l_ref[rows, :] + jnp.sum(pr, axis=1, keepdims=True)
        acc_ref[rows, :] = alpha * acc_ref[rows, :] + pv(pr.astype(BF16))
        m_ref[rows, :] = m_new

    everything = slice(None)

    def b_bias(cols):
        return cqb_scr[...] - _spread_rows(csum_ref[0, :, cols], dec)

    def c_bias(cols):
        return _stack_rows(bias_ref[0, :, cols], H_C)

    @pl.when(p < nsteps)
    def _():
        cols = pl.ds(pl.multiple_of(p * span, span), span)
        ka = jnp.concatenate([r[0, 0] for r in ka_refs], axis=0).astype(BF16)
        va = jnp.concatenate([r[0, 0] for r in va_refs], axis=0).astype(BF16)
        sa = lax.dot_general(qa_scr[...], ka, NT, preferred_element_type=F32)
        rowh = lax.broadcasted_iota(I32, (2 * H_A * dec, 1), 0) // (2 * dec)
        colh = lax.broadcasted_iota(I32, (1, group * H_A * PAGE), 1) % H_A
        sa = jnp.where(colh == rowh, sa, NEG)
        online(states[0], everything, sa, lambda pr: jnp.dot(pr, va, preferred_element_type=F32))
        for state, q_scr, k_refs, v_refs, bias in ((states[1], qb_scr, kb_refs, vb_refs, b_bias),
                                                   (states[2], qc_scr, kc_refs, vc_refs, c_bias)):
            q = q_scr[...]
            s = jnp.concatenate([jnp.dot(q, r[0, 0].reshape(wide, PAGE).astype(BF16), preferred_element_type=F32)
                                 for r in k_refs], axis=1) + bias(cols)

            def pv(pr, v_refs=v_refs):
                out = None
                for g, r in enumerate(v_refs):
                    part = lax.dot_general(pr[:, g * LANES:(g + 1) * LANES], r[0, 0].reshape(wide, PAGE).astype(BF16),
                                           NT, preferred_element_type=F32)
                    out = part if out is None else out + part
                return out

            online(state, everything, s, pv)

    @pl.when(p == nsteps)
    def _():
        tail = slice(pages * LANES, (pages + 1) * LANES)

        def causal(nrows):
            qrow = lax.broadcasted_iota(I32, (nrows, LANES), 0) % dec
            return lax.broadcasted_iota(I32, (nrows, LANES), 1) <= qrow

        for h in range(H_A):
            rows = slice(2 * dec * h, 2 * dec * (h + 1))
            kn = _pad_rows(_cols(proj_ref, (AK + h) * LANES, LANES).astype(BF16), LANES)
            vn = _pad_rows(_cols(proj_ref, (AV + h) * LANES, LANES).astype(BF16), LANES)
            s = lax.dot_general(qa_scr[rows, :], kn, NT, preferred_element_type=F32)
            s = jnp.where(causal(2 * dec), s, NEG)
            online(states[0], rows, s, lambda pr, vn=vn: jnp.dot(pr, vn, preferred_element_type=F32))
        for state, q_scr, kcol, vcol, bias, masked in ((states[1], qb_scr, BK, BV, b_bias, True),
                                                       (states[2], qc_scr, CK, CV, c_bias, False)):
            kn = _pad_rows(proj_ref[:, kcol * LANES:kcol * LANES + wide].astype(BF16), LANES)
            vn = _pad_rows(proj_ref[:, vcol * LANES:vcol * LANES + wide].astype(BF16), LANES)
            s = lax.dot_general(q_scr[...], kn, NT, preferred_element_type=F32) + bias(tail)
            if masked:
                s = jnp.where(causal(H_B * dec), s, NEG)
            online(state, everything, s, lambda pr, vn=vn: jnp.dot(pr, vn, preferred_element_type=F32))

        lam, lam_init = _diff_lambda(lam_ref, layer)
        outs = []
        for h in range(H_A):
            r0 = slice(2 * dec * h, 2 * dec * h + dec)
            r1 = slice(2 * dec * h + dec, 2 * dec * (h + 1))
            outs.append(_diff_finish(acca[r0, :] / la[r0, :], acca[r1, :] / la[r1, :], lam, lam_init, g_ref[...]))
        oa_ref[...] = jnp.concatenate(outs, axis=1)
        for o_ref, l_ref, acc_ref in ((ob_ref, lb, accb), (oc_ref, lc, accc)):
            out = jnp.zeros((dec, wide), F32)
            for h in range(H_B):
                rows = slice(dec * h, dec * (h + 1))
                out = out + jnp.where(colhead == h, acc_ref[rows, :] / l_ref[rows, :], 0.0)
            o_ref[...] = out


def _sample_attn(pt_flat, p32s, caches, csum_t, cq, bias, lam, g, layer, nseq, dec, pages):
    width = (pages + 1) * LANES
    wide = H_B * DH
    group = _pick(pages, (4, 2, 1))
    seq_map = lambda b, p, pt: (b, 0, 0)
    in_specs = [pl.BlockSpec((dec, N_MAIN), lambda b, p, pt: (b, 0))]
    operands = []
    for c in caches:
        in_specs += [pl.BlockSpec((1, 1) + c.shape[2:], _page_map(layer, pages, group, gi, c.ndim))
                     for gi in range(group)]
        operands += [c] * group
    in_specs += [
        pl.BlockSpec((1, H_B, width), seq_map),
        pl.BlockSpec((1, dec, H_B), seq_map),
        pl.BlockSpec((1, dec, width), seq_map),
        pl.BlockSpec((4, DH), lambda b, p, pt: (0, 0)),
        pl.BlockSpec((1, 2 * DH), lambda b, p, pt: (0, 0)),
    ]
    rows = H_B * dec
    st = lambda w: [pltpu.VMEM((rows, 1), F32), pltpu.VMEM((rows, 1), F32), pltpu.VMEM((rows, w), F32)]
    grid_spec = pltpu.PrefetchScalarGridSpec(
        num_scalar_prefetch=1,
        grid=(nseq, pages // group + 1),
        in_specs=in_specs,
        out_specs=[pl.BlockSpec((dec, 4 * LANES), lambda b, p, pt: (b, 0))] * 3,
        scratch_shapes=[pltpu.VMEM((rows, LANES), BF16), pltpu.VMEM((rows, wide), BF16), pltpu.VMEM((rows, wide), BF16),
                        pltpu.VMEM((rows, 1), F32)] + st(2 * DH) + st(wide) + st(wide),
    )
    return pl.pallas_call(
        functools.partial(_sample_attn_kernel, pages=pages, group=group, layer=layer, dec=dec),
        grid_spec=grid_spec,
        out_shape=[jax.ShapeDtypeStruct((nseq * dec, 4 * LANES), F32)] * 3,
        compiler_params=_cparams(("parallel", "arbitrary")),
        name="sample_attention",
    )(pt_flat, p32s, *operands, csum_t, cq, bias, lam, g)


def _layer_norm(h, g, b):
    mu = jnp.mean(h, axis=-1, keepdims=True)
    var = jnp.mean(jnp.square(h - mu), axis=-1, keepdims=True)
    return (h - mu) * lax.rsqrt(var + LN_EPS) * g + b


def _merge_kernel(oa_ref, ob_ref, oc_ref, g_ref, x_ref, wa_ref, wb_ref, wc_ref, wo_ref, lg_ref, lb_ref,
                  wr_ref, br_ref, x1_ref, x16_ref, gate_ref, *, alpha):
    ya = jnp.dot(oa_ref[...], wa_ref[...], preferred_element_type=F32)
    yb = jnp.dot(ob_ref[...], wb_ref[...], preferred_element_type=F32)
    yc = jnp.dot(oc_ref[...], wc_ref[...], preferred_element_type=F32)
    merged = (g_ref[:, 0:D_MODEL] * ya + g_ref[:, D_MODEL:2 * D_MODEL] * yb
              + g_ref[:, 2 * D_MODEL:3 * D_MODEL] * yc)
    mix = jnp.dot(merged.astype(BF16), wo_ref[...], preferred_element_type=F32)
    x1 = _layer_norm(alpha * x_ref[...] + mix, lg_ref[...], lb_ref[...])
    x1_ref[...] = x1
    x16_ref[...] = x1.astype(BF16)
    logits = jnp.dot(x1, wr_ref[...], precision=lax.Precision.HIGHEST, preferred_element_type=F32) + br_ref[...]
    lane = lax.broadcasted_iota(I32, logits.shape, 1)
    work = logits
    vals, hots = [], []
    for _ in range(TOP_K_EXPERTS):
        mx = jnp.max(work, axis=1, keepdims=True)
        ix = jnp.min(jnp.where(work == mx, lane, N_EXPERTS), axis=1, keepdims=True)
        hot = lane == ix
        vals.append(mx)
        hots.append(hot)
        work = jnp.where(hot, -jnp.inf, work)
    es = [jnp.exp(v - vals[0]) for v in vals]
    den = es[0] + es[1] + es[2] + es[3]
    gate = jnp.zeros(logits.shape, F32)
    for e, hot in zip(es, hots):
        gate = gate + jnp.where(hot, e / den, 0.0)
    gate_ref[...] = gate


def _merge(oa, ob, oc, gates, x, wa, wb, wc, wo, lg, lb, wr, br, tm, alpha):
    m = x.shape[0]
    row = lambda w: pl.BlockSpec((tm, w), lambda i: (i, 0))
    full = lambda a: pl.BlockSpec(a.shape, lambda i: (0, 0))
    return pl.pallas_call(
        functools.partial(_merge_kernel, alpha=alpha),
        grid=(m // tm,),
        in_specs=[row(4 * LANES), row(4 * LANES), row(4 * LANES), row(3 * D_MODEL), row(D_MODEL),
                  full(wa), full(wb), full(wc), full(wo), full(lg), full(lb), full(wr), full(br)],
        out_specs=[row(D_MODEL), row(D_MODEL), row(N_EXPERTS)],
        out_shape=[jax.ShapeDtypeStruct((m, D_MODEL), F32), jax.ShapeDtypeStruct((m, D_MODEL), BF16),
                   jax.ShapeDtypeStruct((m, N_EXPERTS), F32)],
        compiler_params=_cparams(("parallel",)),
        name="merge_ln_router",
    )(oa, ob, oc, gates, x, wa, wb, wc, wo, lg, lb, wr, br)


MOE_CHUNK = 128


def _moe_kernel(cnt_ref, x16_ref, x1_ref, gate_ref, gatet_ref, w1g_ref, w1l_ref, b1g_ref, b1l_ref, w2_ref, b2_ref,
                lg_ref, lb_ref, o_ref, acc_scr, rank_scr, rankt_scr, *, alpha, tm):
    i = pl.program_id(0)
    e = pl.program_id(1)

    @pl.when(e == 0)
    def _():
        acc_scr[...] = jnp.zeros(acc_scr.shape, F32)
        r = lax.broadcasted_iota(I32, (tm, tm), 0)
        c = lax.broadcasted_iota(I32, (tm, tm), 1)
        sel = jnp.where(gate_ref[...] > 0.0, 1.0, 0.0).astype(BF16)
        selt = jnp.where(gatet_ref[...] > 0.0, 1.0, 0.0).astype(BF16)
        rank_scr[...] = jnp.dot((c < r).astype(BF16), sel, preferred_element_type=F32)
        rankt_scr[...] = jnp.dot(selt, (r < c).astype(BF16), preferred_element_type=F32)

    hot = lax.broadcasted_iota(I32, (1, N_EXPERTS), 1) == e
    g_col = jnp.sum(jnp.where(hot, gate_ref[...], 0.0), axis=1, keepdims=True)
    r_col = jnp.sum(jnp.where(hot, rank_scr[...], 0.0), axis=1, keepdims=True)
    g_row = gatet_ref[pl.ds(e, 1), :]
    r_row = rankt_scr[pl.ds(e, 1), :]
    slot_r = lax.broadcasted_iota(I32, (MOE_CHUNK, 1), 0).astype(F32)
    slot_c = lax.broadcasted_iota(I32, (1, MOE_CHUNK), 1).astype(F32)
    nch = (cnt_ref[i * N_EXPERTS + e] + MOE_CHUNK - 1) // MOE_CHUNK

    def chunk(ci, _):
        base = (ci * MOE_CHUNK).astype(F32)
        pick = ((r_row - base) == slot_r) & (g_row > 0.0)
        pick_t = ((r_col - base) == slot_c) & (g_col > 0.0)
        xc = jnp.dot(jnp.where(pick, 1.0, 0.0).astype(BF16), x16_ref[...], preferred_element_type=F32).astype(BF16)
        gw = jnp.sum(jnp.where(pick, g_row, 0.0), axis=1, keepdims=True)
        glu = jnp.minimum(jnp.dot(xc, w1g_ref[0], preferred_element_type=F32) + b1g_ref[0], SWIGLU_LIMIT)
        lin = jnp.clip(jnp.dot(xc, w1l_ref[0], preferred_element_type=F32) + b1l_ref[0], -SWIGLU_LIMIT, SWIGLU_LIMIT)
        a = glu * jax.nn.sigmoid(SWIGLU_ALPHA * glu) * (lin + 1.0)
        y = jnp.dot(a.astype(BF16), w2_ref[0], preferred_element_type=F32) + b2_ref[0]
        acc_scr[...] += jnp.dot(jnp.where(pick_t, 1.0, 0.0).astype(BF16), (y * gw).astype(BF16),
                                preferred_element_type=F32)
        return 0

    lax.fori_loop(0, nch, chunk, 0)

    @pl.when(e == N_EXPERTS - 1)
    def _():
        o_ref[...] = _layer_norm(alpha * x1_ref[...] + acc_scr[...], lg_ref[...], lb_ref[...])


def _moe(x16, x1, gate, w1p, b1g, b1l, w2, b2, lg, lb, tm, alpha):
    m = x1.shape[0]
    d_ff = w2.shape[1]
    nt = m // tm
    counts = jnp.sum((gate > 0.0).reshape(nt, tm, N_EXPERTS), axis=1).astype(I32).reshape(-1)
    gate_t = gate.T
    row = lambda w: pl.BlockSpec((tm, w), lambda i, e, c: (i, 0))
    grid_spec = pltpu.PrefetchScalarGridSpec(
        num_scalar_prefetch=1,
        grid=(nt, N_EXPERTS),
        in_specs=[row(D_MODEL), row(D_MODEL), row(N_EXPERTS),
                  pl.BlockSpec((N_EXPERTS, tm), lambda i, e, c: (0, i)),
                  pl.BlockSpec((1, D_MODEL, d_ff), lambda i, e, c: (e, 0, 0)),
                  pl.BlockSpec((1, D_MODEL, d_ff), lambda i, e, c: (e, 0, 1)),
                  pl.BlockSpec((1, 1, d_ff), lambda i, e, c: (e, 0, 0)),
                  pl.BlockSpec((1, 1, d_ff), lambda i, e, c: (e, 0, 0)),
                  pl.BlockSpec((1, d_ff, D_MODEL), lambda i, e, c: (e, 0, 0)),
                  pl.BlockSpec((1, 1, D_MODEL), lambda i, e, c: (e, 0, 0)),
                  pl.BlockSpec((1, D_MODEL), lambda i, e, c: (0, 0)),
                  pl.BlockSpec((1, D_MODEL), lambda i, e, c: (0, 0))],
        out_specs=row(D_MODEL),
        scratch_shapes=[pltpu.VMEM((tm, D_MODEL), F32), pltpu.VMEM((tm, N_EXPERTS), F32),
                        pltpu.VMEM((N_EXPERTS, tm), F32)],
    )
    return pl.pallas_call(
        functools.partial(_moe_kernel, alpha=alpha, tm=tm),
        grid_spec=grid_spec,
        out_shape=jax.ShapeDtypeStruct((m, D_MODEL), F32),
        compiler_params=_cparams(("parallel", "arbitrary")),
        name="moe_ln",
    )(counts, x16, x1, gate, gate_t, w1p, w1p, b1g, b1l, w2, b2, lg, lb)


def _deinterleave_w1(w1):
    f2 = w1.shape[-1]
    f = lax.broadcasted_iota(I32, (f2, f2), 0)
    g = lax.broadcasted_iota(I32, (f2, f2), 1)
    src = jnp.where(g < f2 // 2, 2 * g, 2 * (g - f2 // 2) + 1)
    perm = (f == src).astype(BF16)
    return jnp.einsum("edf,fg->edg", w1.astype(BF16), perm, preferred_element_type=BF16)


def _repack_w_in(w, b):
    offs = np.concatenate([[0], np.cumsum(PROJ_SIZES)])
    seg = lambda a, i: a[..., int(offs[i]):int(offs[i + 1])]
    d = w.shape[0]

    def build(a, rows):
        z = lambda n: jnp.zeros(rows + (n,), a.dtype)
        misc = jnp.concatenate([z(MISC_BF), seg(a, 6), seg(a, 12), z(LANES - MISC_IW - H_I)], axis=-1)
        main = jnp.concatenate([seg(a, 0), seg(a, 1), seg(a, 2), seg(a, 3), seg(a, 4), seg(a, 5),
                                seg(a, 7), seg(a, 8), seg(a, 9), seg(a, 10), seg(a, 11), seg(a, 11), misc], axis=-1)
        return main, seg(a, 13)

    wm, wg = build(w, (d,))
    bm, bg = build(b[None, :], (1,))
    return wm.astype(BF16), bm, wg.astype(BF16), bg


def _rope_tables(pos):
    half = ROT_DIM // 2
    inv = ROPE_THETA ** (-jnp.arange(half, dtype=F32) / half)
    ang = pos.astype(F32)[:, None] * inv
    cos, sin = jnp.cos(ang), jnp.sin(ang)
    m = pos.shape[0]
    ones = jnp.ones((m, DH - ROT_DIM), F32)
    zeros = jnp.zeros((m, DH - ROT_DIM), F32)
    z8 = jnp.zeros((m, half), F32)
    c64 = jnp.concatenate([cos, cos, ones], axis=1)
    a64 = jnp.concatenate([-sin, z8, zeros], axis=1)
    b64 = jnp.concatenate([z8, sin, zeros], axis=1)
    dup = lambda a: jnp.concatenate([a, a], axis=1)
    return dup(c64), dup(a64), dup(b64)


def kernel(x_prompt, x_sample, cache_a_k, cache_a_v, cache_b_k, cache_b_v, cache_b_logf, cache_c_k, cache_c_v,
           cache_c_idx_k, page_table, meta, w_in, b_in, a_lambda, a_norm_g, w_br_a, w_br_b, w_br_c, w_out,
           ln1_g, ln1_b, ln2_g, ln2_b, w_router, b_router, w_mlp1, b_mlp1, w_mlp2, b_mlp2):
    n_p, seq, _ = x_prompt.shape
    nseq, dec, _ = x_sample.shape
    depth = w_in.shape[0]
    pages = page_table.shape[1]
    past_len = pages * PAGE
    t = seq + BLOCK_Q
    topk_prompt = min(TOPK_MAX, seq // 4)
    topk_sample = min(TOPK_MAX, (past_len + dec) // 4)
    alpha = (2 * depth) ** 0.25
    mp = n_p * t
    m_all = mp + nseq * dec
    tm = _pick(m_all, (768, 640, 512, 384, 256, 128, 64, 32, 16, 8))

    xp = jnp.concatenate([jnp.zeros((n_p, META_PAD, D_MODEL), F32),
                          jnp.broadcast_to(meta[None], (n_p, N_META, D_MODEL)), x_prompt], axis=1)
    x = jnp.concatenate([xp.reshape(mp, D_MODEL), x_sample.reshape(nseq * dec, D_MODEL)], axis=0)
    pos = jnp.concatenate([jnp.tile(jnp.arange(t, dtype=I32) - META_PAD, n_p),
                           jnp.tile(past_len + jnp.arange(dec, dtype=I32), nseq)])
    cos, sa, sb = _rope_tables(pos)
    pt_flat = page_table.reshape(-1).astype(I32)
    flat_a = lambda c: c.reshape(c.shape[0], c.shape[1], PAGE * H_A, 2 * DH)
    key_minor = lambda c: jnp.transpose(c, (0, 1, 3, 4, 2))
    caches = (flat_a(cache_a_k), flat_a(cache_a_v), key_minor(cache_b_k), key_minor(cache_b_v),
              key_minor(cache_c_k), key_minor(cache_c_v))
    idx_kt = jnp.transpose(cache_c_idx_k, (0, 1, 3, 2))
    logf_t_cache = jnp.transpose(cache_b_logf, (0, 1, 3, 2))

    rows_p, rows_s = [], []
    for layer in range(depth):
        wm, bm, wg, bg = _repack_w_in(w_in[layer], b_in[layer])
        x16 = x.astype(BF16)
        p32, p16 = _project_main(x16, wm, bm, cos, sa, sb, tm)
        gates = _project_gates(x16, wg, bg, tm)
        lam = a_lambda[layer]
        g = a_norm_g[layer][None, :]

        bf_t = jnp.swapaxes(p32[:mp, MISC * LANES + MISC_BF:MISC * LANES + MISC_BF + H_B].reshape(n_p, t, H_B), 1, 2)
        logf_t, csum_t = _prompt_cumsum(bf_t)
        bias = _prompt_index(p16, p32, n_p, t, topk_prompt)
        oa = _prompt_flash("A", p16, (lam, g), n_p, t, layer)
        ob = _prompt_flash("B", p16, (jnp.swapaxes(csum_t, 1, 2), csum_t), n_p, t, layer)
        oc = _prompt_flash("C", p16, (bias,), n_p, t, layer)

        p32s = p32[mp:]
        sbias, scsum, scq, slogf = _sample_pre(pt_flat, p32s, idx_kt, logf_t_cache, layer, nseq, dec,
                                               pages, topk_sample)
        soa, sob, soc = _sample_attn(pt_flat, p32s, caches, scsum, scq, sbias, lam, g, layer, nseq, dec, pages)

        oa = jnp.concatenate([oa, soa.astype(BF16)], axis=0)
        ob = jnp.concatenate([ob, sob.astype(BF16)], axis=0)
        oc = jnp.concatenate([oc, soc.astype(BF16)], axis=0)
        x1, x1_16, gate = _merge(oa, ob, oc, gates, x,
                                 w_br_a[layer].astype(BF16), w_br_b[layer].astype(BF16), w_br_c[layer].astype(BF16),
                                 w_out[layer].astype(BF16), ln1_g[layer][None], ln1_b[layer][None],
                                 w_router[layer], b_router[layer][None], tm, alpha)
        x = _moe(x1_16, x1, gate, _deinterleave_w1(w_mlp1[layer]),
                 b_mlp1[layer][:, None, 0::2], b_mlp1[layer][:, None, 1::2],
                 w_mlp2[layer].astype(BF16), b_mlp2[layer][:, None, :],
                 ln2_g[layer][None], ln2_b[layer][None], tm, alpha)

        def seg(rows, col, width, shape):
            return rows[:, col * LANES:col * LANES + width].reshape(shape)

        pp = p32[:mp].reshape(n_p, t, N_MAIN)[:, META_PAD:].reshape(n_p * (t - META_PAD), N_MAIN)
        tp = t - META_PAD
        rows_p.append((seg(pp, AK, W_A, (n_p, tp, H_A, 2 * DH)), seg(pp, AV, W_A, (n_p, tp, H_A, 2 * DH)),
                       seg(pp, BK, W_B, (n_p, tp, H_B, DH)), seg(pp, BV, W_B, (n_p, tp, H_B, DH)),
                       jnp.swapaxes(logf_t, 1, 2)[:, META_PAD:],
                       seg(pp, CK, W_C, (n_p, tp, H_C, DH)), seg(pp, CV, W_C, (n_p, tp, H_C, DH)),
                       seg(pp, IK, D_IDX, (n_p, tp, D_IDX))))
        rows_s.append((seg(p32s, AK, W_A, (nseq, dec, H_A, 2 * DH)), seg(p32s, AV, W_A, (nseq, dec, H_A, 2 * DH)),
                       seg(p32s, BK, W_B, (nseq, dec, H_B, DH)), seg(p32s, BV, W_B, (nseq, dec, H_B, DH)),
                       slogf,
                       seg(p32s, CK, W_C, (nseq, dec, H_C, DH)), seg(p32s, CV, W_C, (nseq, dec, H_C, DH)),
                       seg(p32s, IK, D_IDX, (nseq, dec, D_IDX))))

    st = lambda rows, i: jnp.stack([r[i] for r in rows])
    y_prompt = x[:mp].reshape(n_p, t, D_MODEL)[:, BLOCK_Q:]
    y_sample = x[mp:].reshape(nseq, dec, D_MODEL)
    return (y_prompt, y_sample) + tuple(st(rows_p, i) for i in range(8)) + tuple(st(rows_s, i) for i in range(8))
```

```python
import functools
import math

import numpy as np
import jax
import jax.numpy as jnp
from jax import lax
from jax.experimental import pallas as pl
from jax.experimental.pallas import tpu as pltpu

F32 = jnp.float32
BF16 = jnp.bfloat16
I32 = jnp.int32

D_MODEL = 1024
DH = 64
H_A = 4
H_B = 8
H_C = 8
H_I = 4
D_IDX = 64
TOPK_MAX = 256
N_META = 16
BLOCK_Q = 128
META_PAD = BLOCK_Q - N_META
ROT_DIM = DH // 4
ROPE_THETA = 500000.0
N_EXPERTS = 32
TOP_K_EXPERTS = 4
SWIGLU_ALPHA = 1.702
SWIGLU_LIMIT = 7.0
LN_EPS = 1e-5
RMS_EPS = 1e-5
NEG = -1e30
M_INIT = float(np.finfo(np.float32).min)
PAGE = 128
LANES = 128
W_A = H_A * 2 * DH
W_B = H_B * DH
W_C = H_C * DH
PROJ_SIZES = (W_A, W_A, W_A, W_B, W_B, W_B, H_B, W_C, W_C, W_C, H_I * D_IDX, D_IDX, H_I, 3 * D_MODEL)
SCALE = DH ** -0.5
IW_SCALE = H_I ** -0.5 * D_IDX ** -0.5

AQ, AK, AV, BQ, BK, BV, CQ, CK, CV, IQ, IK, MISC = 0, 4, 8, 12, 16, 20, 24, 28, 32, 36, 38, 39
N_MAIN = 40 * LANES
MISC_BF = 16
MISC_IW = 24
PROJ_TN = 512
ROPE_TILES = (1, 1, 0, 0, 0, 0, 1, 1, 0, 1)
VMEM_LIMIT = 56 * 1024 * 1024
INT_MIN = -2 ** 31

NT = (((1,), (1,)), ((), ()))


def _pick(n, cands):
    for c in cands:
        if n % c == 0:
            return c
    raise ValueError(f"no tile for {n}")


def _cparams(sem):
    return pltpu.CompilerParams(dimension_semantics=sem, vmem_limit_bytes=VMEM_LIMIT)


def _log_sigmoid(x):
    return jnp.minimum(x, 0.0) - jnp.log1p(jnp.exp(-jnp.abs(x)))


def _lane_scan(x):
    lane = lax.broadcasted_iota(I32, x.shape, 1)
    s = 1
    while s < LANES:
        x = x + jnp.where(lane >= s, pltpu.roll(x, s, 1), 0.0)
        s *= 2
    return x


def _rope_tile(x, cos, sa, sb):
    outs = []
    for c in range(x.shape[1] // LANES):
        xc = x[:, c * LANES:(c + 1) * LANES]
        up = pltpu.roll(xc, LANES - ROT_DIM // 2, 1)
        dn = pltpu.roll(xc, ROT_DIM // 2, 1)
        outs.append(xc * cos + up * sa + dn * sb)
    return jnp.concatenate(outs, axis=1)


def _proj_kernel(flags_ref, x_ref, w_ref, b_ref, cos_ref, sa_ref, sb_ref, o32_ref, o16_ref):
    j = pl.program_id(1)
    acc = jnp.dot(x_ref[...], w_ref[...], preferred_element_type=F32) + b_ref[...]

    @pl.when(flags_ref[j] == 0)
    def _():
        o32_ref[...] = acc
        o16_ref[...] = acc.astype(BF16)

    @pl.when(flags_ref[j] == 1)
    def _():
        r = _rope_tile(acc, cos_ref[...], sa_ref[...], sb_ref[...])
        o32_ref[...] = r
        o16_ref[...] = r.astype(BF16)


def _project_main(x16, w, b, cos, sa, sb, tm):
    m = x16.shape[0]
    n = w.shape[1]
    flags = jnp.asarray(ROPE_TILES, I32)
    grid_spec = pltpu.PrefetchScalarGridSpec(
        num_scalar_prefetch=1,
        grid=(m // tm, n // PROJ_TN),
        in_specs=[
            pl.BlockSpec((tm, D_MODEL), lambda i, j, f: (i, 0)),
            pl.BlockSpec((D_MODEL, PROJ_TN), lambda i, j, f: (0, j)),
            pl.BlockSpec((1, PROJ_TN), lambda i, j, f: (0, j)),
            pl.BlockSpec((tm, LANES), lambda i, j, f: (i, 0)),
            pl.BlockSpec((tm, LANES), lambda i, j, f: (i, 0)),
            pl.BlockSpec((tm, LANES), lambda i, j, f: (i, 0)),
        ],
        out_specs=[
            pl.BlockSpec((tm, PROJ_TN), lambda i, j, f: (i, j)),
            pl.BlockSpec((tm, PROJ_TN), lambda i, j, f: (i, j)),
        ],
    )
    return pl.pallas_call(
        _proj_kernel,
        grid_spec=grid_spec,
        out_shape=[jax.ShapeDtypeStruct((m, n), F32), jax.ShapeDtypeStruct((m, n), BF16)],
        compiler_params=_cparams(("parallel", "arbitrary")),
        name="proj_main",
    )(flags, x16, w, b, cos, sa, sb)


def _gate_kernel(x_ref, w_ref, b_ref, o_ref):
    acc = jnp.dot(x_ref[...], w_ref[...], preferred_element_type=F32) + b_ref[...]
    o_ref[...] = jax.nn.sigmoid(acc)


def _project_gates(x16, w, b, tm):
    m = x16.shape[0]
    n = w.shape[1]
    return pl.pallas_call(
        _gate_kernel,
        grid=(m // tm, n // PROJ_TN),
        in_specs=[
            pl.BlockSpec((tm, D_MODEL), lambda i, j: (i, 0)),
            pl.BlockSpec((D_MODEL, PROJ_TN), lambda i, j: (0, j)),
            pl.BlockSpec((1, PROJ_TN), lambda i, j: (0, j)),
        ],
        out_specs=pl.BlockSpec((tm, PROJ_TN), lambda i, j: (i, j)),
        out_shape=jax.ShapeDtypeStruct((m, n), F32),
        compiler_params=_cparams(("parallel", "arbitrary")),
        name="proj_gates",
    )(x16, w, b)


def _cumsum_kernel(bf_ref, logf_ref, csum_ref, *, t):
    def body(c, carry):
        sl = pl.ds(pl.multiple_of(c * LANES, LANES), LANES)
        tok = c * LANES + lax.broadcasted_iota(I32, (H_B, LANES), 1)
        lf = jnp.where(tok >= META_PAD, _log_sigmoid(bf_ref[0, :, sl]), 0.0)
        logf_ref[0, :, sl] = lf
        cs = _lane_scan(lf) + carry
        csum_ref[0, :, sl] = cs
        return cs[:, LANES - 1:LANES]

    lax.fori_loop(0, t // LANES, body, jnp.zeros((H_B, 1), F32))


def _prompt_cumsum(bf_t):
    n, h, t = bf_t.shape
    return pl.pallas_call(
        functools.partial(_cumsum_kernel, t=t),
        grid=(n,),
        in_specs=[pl.BlockSpec((1, h, t), lambda b: (b, 0, 0))],
        out_specs=[pl.BlockSpec((1, h, t), lambda b: (b, 0, 0)),
                   pl.BlockSpec((1, h, t), lambda b: (b, 0, 0))],
        out_shape=[jax.ShapeDtypeStruct((n, h, t), F32), jax.ShapeDtypeStruct((n, h, t), F32)],
        compiler_params=_cparams(("parallel",)),
        name="prompt_cumsum",
    )(bf_t)


def _topk_select(sc_ref, key_ref, out_ref, nsuper, cw, k, rows, out_dtype):
    def sl_of(c):
        return pl.ds(pl.multiple_of(c * LANES, LANES), LANES)

    def make_keys(sc, _):
        for jj in range(cw):
            sl = sl_of(sc * cw + jj)
            bits = pltpu.bitcast(sc_ref[:, sl], I32)
            key_ref[:, sl] = jnp.where(bits < 0, bits ^ jnp.int32(0x7FFFFFFF), bits)
        return 0

    lax.fori_loop(0, nsuper, make_keys, 0)

    def count(pred):
        def body(sc, acc):
            for jj in range(cw):
                acc = acc + jnp.where(pred(key_ref[:, sl_of(sc * cw + jj)]), 1, 0).astype(I32)
            return acc
        acc = lax.fori_loop(0, nsuper, body, jnp.zeros((rows, LANES), I32))
        return jnp.sum(acc, axis=1, keepdims=True)

    c0 = count(lambda key: key >= 0)
    t0 = jnp.where(c0 >= k, jnp.int32(0), jnp.int32(INT_MIN))

    def bit_body(i, t):
        cand = t | jnp.left_shift(jnp.int32(1), 30 - i)
        cnt = count(lambda key: key >= cand)
        return jnp.where(cnt >= k, cand, t)

    thr = lax.fori_loop(0, 31, bit_body, t0)
    need = (k - count(lambda key: key > thr)).astype(F32)
    upper = (lax.broadcasted_iota(I32, (LANES, LANES), 0)
             < lax.broadcasted_iota(I32, (LANES, LANES), 1)).astype(BF16)

    def sel_body(sc, carry):
        for jj in range(cw):
            sl = sl_of(sc * cw + jj)
            key = key_ref[:, sl]
            eq = key == thr
            eqf = jnp.where(eq, 1.0, 0.0)
            rank = jnp.dot(eqf.astype(BF16), upper, preferred_element_type=F32) + carry
            sel = (key > thr) | (eq & (rank < need))
            valid = sc_ref[:, sl] > 0.5 * NEG
            out_ref[:, sl] = jnp.where(sel & valid, 0.0, NEG).astype(out_dtype)
            carry = carry + jnp.sum(eqf, axis=1, keepdims=True)
        return carry

    lax.fori_loop(0, nsuper, sel_body, jnp.zeros((rows, 1), F32))


def _prompt_index_kernel(iq_ref, ik_ref, misc_ref, bias_ref, sc_scr, key_scr, *, k, cw):
    i = pl.program_id(1)
    wide = cw * LANES
    nsuper = (i + cw) // cw
    lo = lax.broadcasted_iota(I32, (1, LANES), 1) < DH
    zero = jnp.zeros((BLOCK_Q, LANES), BF16)
    iq_heads = []
    for pair in range(H_I // 2):
        blk = iq_ref[:, pair * LANES:(pair + 1) * LANES]
        iq_heads.append(jnp.where(lo, blk, zero))
        iq_heads.append(jnp.where(lo, zero, blk))
    iw = misc_ref[...][:, MISC_IW:MISC_IW + H_I] * IW_SCALE
    iw_wide = [jnp.broadcast_to(iw[:, h:h + 1], (BLOCK_Q, wide)) for h in range(H_I)]
    qidx = i * BLOCK_Q + lax.broadcasted_iota(I32, (BLOCK_Q, 1), 0)

    def chunk(c, _):
        sl = pl.ds(pl.multiple_of(c * wide, wide), wide)
        kmat = ik_ref[sl, :]
        acc = None
        for h in range(H_I):
            s = lax.dot_general(iq_heads[h], kmat, NT, preferred_element_type=F32)
            term = jnp.maximum(s, 0.0) * iw_wide[h]
            acc = term if acc is None else acc + term
        kidx = c * wide + lax.broadcasted_iota(I32, (1, wide), 1)
        ok = (kidx <= qidx) & (kidx >= META_PAD)
        sc_scr[:, sl] = jnp.where(ok, acc, NEG)
        return 0

    lax.fori_loop(0, nsuper, chunk, 0)
    bias_ref[...] = jnp.full(bias_ref.shape, NEG, bias_ref.dtype)
    _topk_select(sc_scr, key_scr, bias_ref.at[0], nsuper, cw, k, BLOCK_Q, bias_ref.dtype)


def _prompt_index(p16, p32, n, t, k):
    nq = t // BLOCK_Q
    return pl.pallas_call(
        functools.partial(_prompt_index_kernel, k=k, cw=_pick(t, (640, 512, 384, 256, 128)) // LANES),
        grid=(n, nq),
        in_specs=[
            pl.BlockSpec((BLOCK_Q, 2 * LANES), lambda b, i: (b * nq + i, IQ // 2)),
            pl.BlockSpec((t, LANES), lambda b, i: (b, IK)),
            pl.BlockSpec((BLOCK_Q, LANES), lambda b, i: (b * nq + i, MISC)),
        ],
        out_specs=pl.BlockSpec((1, BLOCK_Q, t), lambda b, i: (b, i, 0)),
        out_shape=jax.ShapeDtypeStruct((n, t, t), BF16),
        scratch_shapes=[pltpu.VMEM((BLOCK_Q, t), F32), pltpu.VMEM((BLOCK_Q, t), I32)],
        compiler_params=_cparams(("parallel", "arbitrary")),
        name="prompt_index_topk",
    )(p16, p16, p32)


def _diff_lambda(lam_ref, layer):
    lam_init = 0.8 - 0.6 * math.exp(-0.3 * layer)
    lv = lam_ref[...]
    a = jnp.sum(lv[0:1] * lv[1:2], axis=1, keepdims=True)
    b = jnp.sum(lv[2:3] * lv[3:4], axis=1, keepdims=True)
    return jnp.exp(a) - jnp.exp(b) + lam_init, lam_init


def _diff_finish(o0, o1, lam, lam_init, g):
    o = o0 - lam * o1
    o = o * lax.rsqrt(jnp.mean(jnp.square(o), axis=-1, keepdims=True) + RMS_EPS)
    return o * g * (1.0 - lam_init)


def _flash_kernel(qi_ref, kj_ref, *refs, mode, tq, tk, layer):
    if mode == "A":
        q_ref, k_ref, v_ref, lam_ref, g_ref, o_ref, m_scr, acc_scr = refs
    elif mode == "B":
        q_ref, k_ref, v_ref, cq_ref, ck_ref, o_ref, m_scr, acc_scr, cq_scr = refs
    else:
        q_ref, k_ref, v_ref, bias_ref, o_ref, m_scr, acc_scr = refs
    cb = pl.program_id(1)
    step = pl.program_id(2)
    i = qi_ref[step]
    j = kj_ref[step]
    lo = lax.broadcasted_iota(I32, (1, LANES), 1) < DH

    @pl.when(j == 0)
    def _():
        m_scr[...] = jnp.full(m_scr.shape, M_INIT, F32)
        acc_scr[...] = jnp.zeros(acc_scr.shape, F32)
        if mode == "B":
            lane8 = lax.broadcasted_iota(I32, (1, H_B), 1)
            for u in range(2):
                cq_scr[u] = jnp.sum(jnp.where(lane8 == 2 * cb + u, cq_ref[0], 0.0), axis=1, keepdims=True)

    def update(masked):
        q = q_ref[...] * jnp.asarray(SCALE, BF16)
        zero = jnp.zeros_like(q)
        k = k_ref[...]
        v_ext = jnp.concatenate([v_ref[...], jnp.ones((tk, LANES), BF16)], axis=1)
        if masked:
            qidx = i * tq + lax.broadcasted_iota(I32, (tq, 1), 0)
            kidx = j * tk + lax.broadcasted_iota(I32, (1, tk), 1)
            ok = (kidx <= qidx) & (kidx >= META_PAD)
        for u in range(2):
            qu = jnp.where(lo, q, zero) if u == 0 else jnp.where(lo, zero, q)
            s = lax.dot_general(qu, k, NT, preferred_element_type=F32)
            if mode == "B":
                s = s + cq_scr[u] - ck_ref[0, pl.ds(2 * cb + u, 1), :]
            if mode == "C":
                s = s + bias_ref[0].astype(F32)
            if masked:
                s = jnp.where(ok, s, NEG)
            m_prev = m_scr[u]
            m_new = jnp.maximum(m_prev, jnp.max(s, axis=1, keepdims=True))
            alpha = jnp.exp(m_prev - m_new)
            p = jnp.exp((s - m_new).astype(BF16))
            acc_scr[u] = alpha * acc_scr[u] + jnp.dot(p, v_ext, preferred_element_type=F32)
            m_scr[u] = m_new

    if mode == "C":
        update(False)
    else:
        edge = (j == i) | (j == 0)

        @pl.when(edge)
        def _():
            update(True)

        @pl.when(jnp.logical_not(edge))
        def _():
            update(False)

    @pl.when(j == i)
    def _():
        o0 = acc_scr[0, :, 0:LANES] / acc_scr[0, :, LANES:LANES + 1]
        o1 = acc_scr[1, :, 0:LANES] / acc_scr[1, :, LANES:LANES + 1]
        if mode == "A":
            lam, lam_init = _diff_lambda(lam_ref, layer)
            o_ref[...] = _diff_finish(o0, o1, lam, lam_init, g_ref[...]).astype(o_ref.dtype)
        else:
            o_ref[...] = jnp.where(lo, o0, o1).astype(o_ref.dtype)


def _prompt_flash(mode, p16, extras, n, t, layer):
    tq = _pick(t, (640, 512, 384, 256, 128))
    tk = tq
    nq = t // tq
    qi = np.array([i for i in range(nq) for _ in range(i + 1)], np.int32)
    kj = np.array([j for i in range(nq) for j in range(i + 1)], np.int32)
    qcol, kcol, vcol = {"A": (AQ, AK, AV), "B": (BQ, BK, BV), "C": (CQ, CK, CV)}[mode]
    in_specs = [
        pl.BlockSpec((tq, LANES), lambda b, c, s, qi, kj: (b * nq + qi[s], qcol + c)),
        pl.BlockSpec((tk, LANES), lambda b, c, s, qi, kj: (b * nq + kj[s], kcol + c)),
        pl.BlockSpec((tk, LANES), lambda b, c, s, qi, kj: (b * nq + kj[s], vcol + c)),
    ]
    scratch = [pltpu.VMEM((2, tq, 1), F32), pltpu.VMEM((2, tq, 2 * LANES), F32)]
    if mode == "A":
        in_specs += [pl.BlockSpec((4, DH), lambda b, c, s, qi, kj: (0, 0)),
                     pl.BlockSpec((1, 2 * DH), lambda b, c, s, qi, kj: (0, 0))]
    elif mode == "B":
        in_specs += [pl.BlockSpec((1, tq, H_B), lambda b, c, s, qi, kj: (b, qi[s], 0)),
                     pl.BlockSpec((1, H_B, tk), lambda b, c, s, qi, kj: (b, 0, kj[s]))]
        scratch += [pltpu.VMEM((2, tq, 1), F32)]
    else:
        in_specs += [pl.BlockSpec((1, tq, tk), lambda b, c, s, qi, kj: (b, qi[s], kj[s]))]
    grid_spec = pltpu.PrefetchScalarGridSpec(
        num_scalar_prefetch=2,
        grid=(n, 4, len(qi)),
        in_specs=in_specs,
        out_specs=pl.BlockSpec((tq, LANES), lambda b, c, s, qi, kj: (b * nq + qi[s], c)),
        scratch_shapes=scratch,
    )
    return pl.pallas_call(
        functools.partial(_flash_kernel, mode=mode, tq=tq, tk=tk, layer=layer),
        grid_spec=grid_spec,
        out_shape=jax.ShapeDtypeStruct((n * t, 4 * LANES), BF16),
        compiler_params=_cparams(("parallel", "parallel", "arbitrary")),
        name=f"prompt_flash_{mode}",
    )(jnp.asarray(qi), jnp.asarray(kj), p16, p16, p16, *extras)


def _pad_rows(x, rows):
    return jnp.concatenate([x, jnp.zeros((rows - x.shape[0], x.shape[1]), x.dtype)], axis=0)


def _cols(ref, start, width):
    blk = start // LANES
    off = start - blk * LANES
    x = ref[:, blk * LANES:(blk + 1) * LANES]
    return x if width == LANES else x[:, off:off + width]


def _stack_rows(x, reps):
    return jnp.concatenate([x] * reps, axis=0)


def _spread_rows(x, reps):
    return jnp.concatenate([jnp.broadcast_to(x[h:h + 1], (reps, x.shape[1])) for h in range(x.shape[0])], axis=0)


def _sample_pre_kernel(pt_ref, proj_ref, *refs, pages, group, k, dec):
    ikt_refs, lft_refs = refs[:group], refs[group:2 * group]
    bias_ref, csum_ref, cq_ref, lnew_ref, sc_scr, key_scr, carry_scr = refs[2 * group:]
    p = pl.program_id(1)
    nsteps = pages // group
    span = group * LANES
    iq = jnp.concatenate([_cols(proj_ref, IQ * LANES + h * D_IDX, D_IDX) for h in range(H_I)], axis=0).astype(BF16)
    iw = _cols(proj_ref, MISC * LANES + MISC_IW, H_I) * IW_SCALE
    iw_rows = jnp.concatenate([iw[:, h:h + 1] for h in range(H_I)], axis=0)

    def scores(s):
        w = jnp.maximum(s, 0.0) * iw_rows
        out = w[0:dec]
        for h in range(1, H_I):
            out = out + w[h * dec:(h + 1) * dec]
        return out

    @pl.when(p == 0)
    def _():
        carry_scr[...] = jnp.zeros(carry_scr.shape, F32)

    @pl.when(p < nsteps)
    def _():
        kt = jnp.concatenate([r[0, 0] for r in ikt_refs], axis=1).astype(BF16)
        sc_scr[:, pl.ds(pl.multiple_of(p * span, span), span)] = scores(
            jnp.dot(iq, kt, preferred_element_type=F32))
        scans = [_lane_scan(r[0, 0]) for r in lft_refs]
        carry = carry_scr[...]
        for g, sc in enumerate(scans):
            csum_ref[0, :, pl.ds(pl.multiple_of(p * span + g * LANES, LANES), LANES)] = sc + carry
            carry = carry + sc[:, LANES - 1:LANES]
        carry_scr[...] = carry

    @pl.when(p == nsteps)
    def _():
        tail = slice(pages * LANES, (pages + 1) * LANES)
        knew = _pad_rows(_cols(proj_ref, IK * LANES, D_IDX).astype(BF16), LANES)
        s = scores(lax.dot_general(iq, knew, NT, preferred_element_type=F32))
        row = lax.broadcasted_iota(I32, (dec, LANES), 0)
        lane = lax.broadcasted_iota(I32, (dec, LANES), 1)
        sc_scr[:, tail] = jnp.where(lane <= row, s, NEG)
        lnew = _log_sigmoid(_cols(proj_ref, MISC * LANES + MISC_BF, H_B))
        lnew_ref[0] = lnew
        eye = (lax.broadcasted_iota(I32, (H_B, H_B), 0) == lax.broadcasted_iota(I32, (H_B, H_B), 1)).astype(F32)
        lt = lax.dot_general(eye, _pad_rows(lnew, LANES), NT, precision=lax.Precision.HIGHEST,
                             preferred_element_type=F32)
        cs = _lane_scan(lt) + carry_scr[...]
        csum_ref[0, :, tail] = cs
        pick = (row == lane).astype(F32)
        cq_ref[0] = lax.dot_general(pick, cs, NT, precision=lax.Precision.HIGHEST, preferred_element_type=F32)
        _topk_select(sc_scr, key_scr, bias_ref.at[0], 1, pages + 1, k, dec, F32)


def _page_map(layer, pages, group, g, ndim):
    nsteps = pages // group

    def index_map(b, p, pt):
        return (layer, pt[b * pages + jnp.minimum(p, nsteps - 1) * group + g]) + (0,) * (ndim - 2)
    return index_map


def _sample_pre(pt_flat, p32s, ikt, lft, layer, nseq, dec, pages, k):
    width = (pages + 1) * LANES
    group = _pick(pages, (8, 4, 2, 1))
    in_specs = [pl.BlockSpec((dec, N_MAIN), lambda b, p, pt: (b, 0))]
    in_specs += [pl.BlockSpec((1, 1, D_IDX, PAGE), _page_map(layer, pages, group, g, 4)) for g in range(group)]
    in_specs += [pl.BlockSpec((1, 1, H_B, PAGE), _page_map(layer, pages, group, g, 4)) for g in range(group)]
    grid_spec = pltpu.PrefetchScalarGridSpec(
        num_scalar_prefetch=1,
        grid=(nseq, pages // group + 1),
        in_specs=in_specs,
        out_specs=[
            pl.BlockSpec((1, dec, width), lambda b, p, pt: (b, 0, 0)),
            pl.BlockSpec((1, H_B, width), lambda b, p, pt: (b, 0, 0)),
            pl.BlockSpec((1, dec, H_B), lambda b, p, pt: (b, 0, 0)),
            pl.BlockSpec((1, dec, H_B), lambda b, p, pt: (b, 0, 0)),
        ],
        scratch_shapes=[pltpu.VMEM((dec, width), F32), pltpu.VMEM((dec, width), I32), pltpu.VMEM((H_B, 1), F32)],
    )
    return pl.pallas_call(
        functools.partial(_sample_pre_kernel, pages=pages, group=group, k=k, dec=dec),
        grid_spec=grid_spec,
        out_shape=[jax.ShapeDtypeStruct((nseq, dec, width), F32),
                   jax.ShapeDtypeStruct((nseq, H_B, width), F32),
                   jax.ShapeDtypeStruct((nseq, dec, H_B), F32),
                   jax.ShapeDtypeStruct((nseq, dec, H_B), F32)],
        compiler_params=_cparams(("parallel", "arbitrary")),
        name="sample_index_topk",
    )(pt_flat, p32s, *([ikt] * group), *([lft] * group))


def _sample_attn_kernel(pt_ref, proj_ref, *refs, pages, group, layer, dec):
    caches = [refs[c * group:(c + 1) * group] for c in range(6)]
    ka_refs, va_refs, kb_refs, vb_refs, kc_refs, vc_refs = caches
    (csum_ref, cq_ref, bias_ref, lam_ref, g_ref, oa_ref, ob_ref, oc_ref,
     qa_scr, qb_scr, qc_scr, cqb_scr, ma, la, acca, mb, lb, accb, mc, lc, accc) = refs[6 * group:]
    nsteps = pages // group
    span = group * LANES
    p = pl.program_id(1)
    wide = H_B * DH
    lo = lax.broadcasted_iota(I32, (1, LANES), 1) < DH
    colhead = lax.broadcasted_iota(I32, (1, wide), 1) // DH
    states = ((ma, la, acca), (mb, lb, accb), (mc, lc, accc))

    @pl.when(p == 0)
    def _():
        for m_ref, l_ref, acc_ref in states:
            m_ref[...] = jnp.full(m_ref.shape, M_INIT, F32)
            l_ref[...] = jnp.zeros(l_ref.shape, F32)
            acc_ref[...] = jnp.zeros(acc_ref.shape, F32)
        pieces = []
        for h in range(H_A):
            qh = _cols(proj_ref, (AQ + h) * LANES, LANES) * SCALE
            pieces += [jnp.where(lo, qh, 0.0), jnp.where(lo, 0.0, qh)]
        qa_scr[...] = jnp.concatenate(pieces, axis=0).astype(BF16)
        for q_scr, col in ((qb_scr, BQ), (qc_scr, CQ)):
            q_all = proj_ref[:, col * LANES:col * LANES + wide] * SCALE
            q_scr[...] = jnp.concatenate([jnp.where(colhead == h, q_all, 0.0) for h in range(H_B)],
                                         axis=0).astype(BF16)
        cq = cq_ref[0]
        cqb_scr[...] = jnp.concatenate([cq[:, h:h + 1] for h in range(H_B)], axis=0)

    def online(state, rows, s, pv):
        m_ref, l_ref, acc_ref = state
        m_prev = m_ref[rows, :]
        m_new = jnp.maximum(m_prev, jnp.max(s, axis=1, keepdims=True))
        alpha = jnp.exp(m_prev - m_new)
        pr = jnp.exp(s - m_new)
        l_ref[rows, :] = alpha * l_ref[rows, :] + jnp.sum(pr, axis=1, keepdims=True)
        acc_ref[rows, :] = alpha * acc_ref[rows, :] + pv(pr.astype(BF16))
        m_ref[rows, :] = m_new

    everything = slice(None)

    def b_bias(cols):
        return cqb_scr[...] - _spread_rows(csum_ref[0, :, cols], dec)

    def c_bias(cols):
        return _stack_rows(bias_ref[0, :, cols], H_C)

    @pl.when(p < nsteps)
    def _():
        cols = pl.ds(pl.multiple_of(p * span, span), span)
        ka = jnp.concatenate([r[0, 0] for r in ka_refs], axis=0).astype(BF16)
        va = jnp.concatenate([r[0, 0] for r in va_refs], axis=0).astype(BF16)
        sa = lax.dot_general(qa_scr[...], ka, NT, preferred_element_type=F32)
        rowh = lax.broadcasted_iota(I32, (2 * H_A * dec, 1), 0) // (2 * dec)
        colh = lax.broadcasted_iota(I32, (1, group * H_A * PAGE), 1) % H_A
        sa = jnp.where(colh == rowh, sa, NEG)
        online(states[0], everything, sa, lambda pr: jnp.dot(pr, va, preferred_element_type=F32))
        for state, q_scr, k_refs, v_refs, bias in ((states[1], qb_scr, kb_refs, vb_refs, b_bias),
                                                   (states[2], qc_scr, kc_refs, vc_refs, c_bias)):
            q = q_scr[...]
            s = jnp.concatenate([jnp.dot(q, r[0, 0].reshape(wide, PAGE).astype(BF16), preferred_element_type=F32)
                                 for r in k_refs], axis=1) + bias(cols)

            def pv(pr, v_refs=v_refs):
                out = None
                for g, r in enumerate(v_refs):
                    part = lax.dot_general(pr[:, g * LANES:(g + 1) * LANES], r[0, 0].reshape(wide, PAGE).astype(BF16),
                                           NT, preferred_element_type=F32)
                    out = part if out is None else out + part
                return out

            online(state, everything, s, pv)

    @pl.when(p == nsteps)
    def _():
        tail = slice(pages * LANES, (pages + 1) * LANES)

        def causal(nrows):
            qrow = lax.broadcasted_iota(I32, (nrows, LANES), 0) % dec
            return lax.broadcasted_iota(I32, (nrows, LANES), 1) <= qrow

        for h in range(H_A):
            rows = slice(2 * dec * h, 2 * dec * (h + 1))
            kn = _pad_rows(_cols(proj_ref, (AK + h) * LANES, LANES).astype(BF16), LANES)
            vn = _pad_rows(_cols(proj_ref, (AV + h) * LANES, LANES).astype(BF16), LANES)
            s = lax.dot_general(qa_scr[rows, :], kn, NT, preferred_element_type=F32)
            s = jnp.where(causal(2 * dec), s, NEG)
            online(states[0], rows, s, lambda pr, vn=vn: jnp.dot(pr, vn, preferred_element_type=F32))
        for state, q_scr, kcol, vcol, bias, masked in ((states[1], qb_scr, BK, BV, b_bias, True),
                                                       (states[2], qc_scr, CK, CV, c_bias, False)):
            kn = _pad_rows(proj_ref[:, kcol * LANES:kcol * LANES + wide].astype(BF16), LANES)
            vn = _pad_rows(proj_ref[:, vcol * LANES:vcol * LANES + wide].astype(BF16), LANES)
            s = lax.dot_general(q_scr[...], kn, NT, preferred_element_type=F32) + bias(tail)
            if masked:
                s = jnp.where(causal(H_B * dec), s, NEG)
            online(state, everything, s, lambda pr, vn=vn: jnp.dot(pr, vn, preferred_element_type=F32))

        lam, lam_init = _diff_lambda(lam_ref, layer)
        outs = []
        for h in range(H_A):
            r0 = slice(2 * dec * h, 2 * dec * h + dec)
            r1 = slice(2 * dec * h + dec, 2 * dec * (h + 1))
            outs.append(_diff_finish(acca[r0, :] / la[r0, :], acca[r1, :] / la[r1, :], lam, lam_init, g_ref[...]))
        oa_ref[...] = jnp.concatenate(outs, axis=1)
        for o_ref, l_ref, acc_ref in ((ob_ref, lb, accb), (oc_ref, lc, accc)):
            out = jnp.zeros((dec, wide), F32)
            for h in range(H_B):
                rows = slice(dec * h, dec * (h + 1))
                out = out + jnp.where(colhead == h, acc_ref[rows, :] / l_ref[rows, :], 0.0)
            o_ref[...] = out


def _sample_attn(pt_flat, p32s, caches, csum_t, cq, bias, lam, g, layer, nseq, dec, pages):
    width = (pages + 1) * LANES
    wide = H_B * DH
    group = _pick(pages, (4, 2, 1))
    seq_map = lambda b, p, pt: (b, 0, 0)
    in_specs = [pl.BlockSpec((dec, N_MAIN), lambda b, p, pt: (b, 0))]
    operands = []
    for c in caches:
        in_specs += [pl.BlockSpec((1, 1) + c.shape[2:], _page_map(layer, pages, group, gi, c.ndim))
                     for gi in range(group)]
        operands += [c] * group
    in_specs += [
        pl.BlockSpec((1, H_B, width), seq_map),
        pl.BlockSpec((1, dec, H_B), seq_map),
        pl.BlockSpec((1, dec, width), seq_map),
        pl.BlockSpec((4, DH), lambda b, p, pt: (0, 0)),
        pl.BlockSpec((1, 2 * DH), lambda b, p, pt: (0, 0)),
    ]
    rows = H_B * dec
    st = lambda w: [pltpu.VMEM((rows, 1), F32), pltpu.VMEM((rows, 1), F32), pltpu.VMEM((rows, w), F32)]
    grid_spec = pltpu.PrefetchScalarGridSpec(
        num_scalar_prefetch=1,
        grid=(nseq, pages // group + 1),
        in_specs=in_specs,
        out_specs=[pl.BlockSpec((dec, 4 * LANES), lambda b, p, pt: (b, 0))] * 3,
        scratch_shapes=[pltpu.VMEM((rows, LANES), BF16), pltpu.VMEM((rows, wide), BF16), pltpu.VMEM((rows, wide), BF16),
                        pltpu.VMEM((rows, 1), F32)] + st(2 * DH) + st(wide) + st(wide),
    )
    return pl.pallas_call(
        functools.partial(_sample_attn_kernel, pages=pages, group=group, layer=layer, dec=dec),
        grid_spec=grid_spec,
        out_shape=[jax.ShapeDtypeStruct((nseq * dec, 4 * LANES), F32)] * 3,
        compiler_params=_cparams(("parallel", "arbitrary")),
        name="sample_attention",
    )(pt_flat, p32s, *operands, csum_t, cq, bias, lam, g)


def _layer_norm(h, g, b):
    mu = jnp.mean(h, axis=-1, keepdims=True)
    var = jnp.mean(jnp.square(h - mu), axis=-1, keepdims=True)
    return (h - mu) * lax.rsqrt(var + LN_EPS) * g + b


def _merge_kernel(oa_ref, ob_ref, oc_ref, g_ref, x_ref, wa_ref, wb_ref, wc_ref, wo_ref, lg_ref, lb_ref,
                  wr_ref, br_ref, x1_ref, x16_ref, gate_ref, *, alpha):
    ya = jnp.dot(oa_ref[...], wa_ref[...], preferred_element_type=F32)
    yb = jnp.dot(ob_ref[...], wb_ref[...], preferred_element_type=F32)
    yc = jnp.dot(oc_ref[...], wc_ref[...], preferred_element_type=F32)
    merged = (g_ref[:, 0:D_MODEL] * ya + g_ref[:, D_MODEL:2 * D_MODEL] * yb
              + g_ref[:, 2 * D_MODEL:3 * D_MODEL] * yc)
    mix = jnp.dot(merged.astype(BF16), wo_ref[...], preferred_element_type=F32)
    x1 = _layer_norm(alpha * x_ref[...] + mix, lg_ref[...], lb_ref[...])
    x1_ref[...] = x1
    x16_ref[...] = x1.astype(BF16)
    logits = jnp.dot(x1, wr_ref[...], precision=lax.Precision.HIGHEST, preferred_element_type=F32) + br_ref[...]
    lane = lax.broadcasted_iota(I32, logits.shape, 1)
    work = logits
    vals, hots = [], []
    for _ in range(TOP_K_EXPERTS):
        mx = jnp.max(work, axis=1, keepdims=True)
        ix = jnp.min(jnp.where(work == mx, lane, N_EXPERTS), axis=1, keepdims=True)
        hot = lane == ix
        vals.append(mx)
        hots.append(hot)
        work = jnp.where(hot, -jnp.inf, work)
    es = [jnp.exp(v - vals[0]) for v in vals]
    den = es[0] + es[1] + es[2] + es[3]
    gate = jnp.zeros(logits.shape, F32)
    for e, hot in zip(es, hots):
        gate = gate + jnp.where(hot, e / den, 0.0)
    gate_ref[...] = gate


def _merge(oa, ob, oc, gates, x, wa, wb, wc, wo, lg, lb, wr, br, tm, alpha):
    m = x.shape[0]
    row = lambda w: pl.BlockSpec((tm, w), lambda i: (i, 0))
    full = lambda a: pl.BlockSpec(a.shape, lambda i: (0, 0))
    return pl.pallas_call(
        functools.partial(_merge_kernel, alpha=alpha),
        grid=(m // tm,),
        in_specs=[row(4 * LANES), row(4 * LANES), row(4 * LANES), row(3 * D_MODEL), row(D_MODEL),
                  full(wa), full(wb), full(wc), full(wo), full(lg), full(lb), full(wr), full(br)],
        out_specs=[row(D_MODEL), row(D_MODEL), row(N_EXPERTS)],
        out_shape=[jax.ShapeDtypeStruct((m, D_MODEL), F32), jax.ShapeDtypeStruct((m, D_MODEL), BF16),
                   jax.ShapeDtypeStruct((m, N_EXPERTS), F32)],
        compiler_params=_cparams(("parallel",)),
        name="merge_ln_router",
    )(oa, ob, oc, gates, x, wa, wb, wc, wo, lg, lb, wr, br)


MOE_CHUNK = 256
MOE_ROWS = 1536


def _moe_kernel(cnt_ref, x16_ref, gate_ref, gatet_ref, w1g_ref, w1l_ref, b1g_ref, b1l_ref, w2_ref, b2_ref,
                o_ref, rank_scr, rankt_scr, *, tm):
    i = pl.program_id(0)
    e = pl.program_id(1)

    @pl.when(e == 0)
    def _():
        o_ref[...] = jnp.zeros(o_ref.shape, F32)
        r = lax.broadcasted_iota(I32, (LANES, LANES), 0)
        c = lax.broadcasted_iota(I32, (LANES, LANES), 1)
        lower = (c < r).astype(BF16)
        upper = (r < c).astype(BF16)
        carry = jnp.zeros((1, N_EXPERTS), F32)
        carry_t = jnp.zeros((N_EXPERTS, 1), F32)
        for blk in range(tm // LANES):
            rows = slice(blk * LANES, (blk + 1) * LANES)
            sel = jnp.where(gate_ref[rows, :] > 0.0, 1.0, 0.0)
            selt = jnp.where(gatet_ref[:, rows] > 0.0, 1.0, 0.0)
            rank_scr[rows, :] = jnp.dot(lower, sel.astype(BF16), preferred_element_type=F32) + carry
            rankt_scr[:, rows] = jnp.dot(selt.astype(BF16), upper, preferred_element_type=F32) + carry_t
            carry = carry + jnp.sum(sel, axis=0, keepdims=True)
            carry_t = carry_t + jnp.sum(selt, axis=1, keepdims=True)

    hot = lax.broadcasted_iota(I32, (1, N_EXPERTS), 1) == e
    g_col = jnp.sum(jnp.where(hot, gate_ref[...], 0.0), axis=1, keepdims=True)
    r_col = jnp.sum(jnp.where(hot, rank_scr[...], 0.0), axis=1, keepdims=True)
    g_row = gatet_ref[pl.ds(e, 1), :]
    r_row = rankt_scr[pl.ds(e, 1), :]
    slot_r = lax.broadcasted_iota(I32, (MOE_CHUNK, 1), 0).astype(F32)
    slot_c = lax.broadcasted_iota(I32, (1, MOE_CHUNK), 1).astype(F32)
    nch = (cnt_ref[i * N_EXPERTS + e] + MOE_CHUNK - 1) // MOE_CHUNK

    def chunk(ci, _):
        base = (ci * MOE_CHUNK).astype(F32)
        pick = ((r_row - base) == slot_r) & (g_row > 0.0)
        pick_t = ((r_col - base) == slot_c) & (g_col > 0.0)
        xc = jnp.dot(jnp.where(pick, 1.0, 0.0).astype(BF16), x16_ref[...], preferred_element_type=F32).astype(BF16)
        gw = jnp.sum(jnp.where(pick, g_row, 0.0), axis=1, keepdims=True)
        glu = jnp.minimum(jnp.dot(xc, w1g_ref[0], preferred_element_type=F32) + b1g_ref[0], SWIGLU_LIMIT)
        lin = jnp.clip(jnp.dot(xc, w1l_ref[0], preferred_element_type=F32) + b1l_ref[0], -SWIGLU_LIMIT, SWIGLU_LIMIT)
        a = glu * jax.nn.sigmoid(SWIGLU_ALPHA * glu) * (lin + 1.0)
        y = jnp.dot(a.astype(BF16), w2_ref[0], preferred_element_type=F32) + b2_ref[0]
        o_ref[...] += jnp.dot(jnp.where(pick_t, 1.0, 0.0).astype(BF16), (y * gw).astype(BF16),
                              preferred_element_type=F32)
        return 0

    lax.fori_loop(0, nch, chunk, 0)


def _residual_ln_kernel(x_ref, y_ref, lg_ref, lb_ref, o_ref, *, alpha):
    o_ref[...] = _layer_norm(alpha * x_ref[...] + y_ref[...], lg_ref[...], lb_ref[...])


def _residual_ln(x, y, lg, lb, tm, alpha):
    m = x.shape[0]
    row = pl.BlockSpec((tm, D_MODEL), lambda i: (i, 0))
    vec = pl.BlockSpec((1, D_MODEL), lambda i: (0, 0))
    return pl.pallas_call(
        functools.partial(_residual_ln_kernel, alpha=alpha),
        grid=(m // tm,),
        in_specs=[row, row, vec, vec],
        out_specs=row,
        out_shape=jax.ShapeDtypeStruct((m, D_MODEL), F32),
        compiler_params=_cparams(("parallel",)),
        name="residual_ln",
    )(x, y, lg, lb)


def _moe(x16, x1, gate, w1p, b1g, b1l, w2, b2, lg, lb, tm_rows, alpha):
    m = x1.shape[0]
    d_ff = w2.shape[1]
    tm = min(MOE_ROWS, -(-m // LANES) * LANES)
    nt = -(-m // tm)
    pad = nt * tm - m
    x16 = jnp.pad(x16, ((0, pad), (0, 0)))
    gate = jnp.pad(gate, ((0, pad), (0, 0)))
    counts = jnp.sum((gate > 0.0).reshape(nt, tm, N_EXPERTS), axis=1).astype(I32).reshape(-1)
    gate_t = gate.T
    row = lambda w: pl.BlockSpec((tm, w), lambda i, e, c: (i, 0))
    grid_spec = pltpu.PrefetchScalarGridSpec(
        num_scalar_prefetch=1,
        grid=(nt, N_EXPERTS),
        in_specs=[row(D_MODEL), row(N_EXPERTS),
                  pl.BlockSpec((N_EXPERTS, tm), lambda i, e, c: (0, i)),
                  pl.BlockSpec((1, D_MODEL, d_ff), lambda i, e, c: (e, 0, 0)),
                  pl.BlockSpec((1, D_MODEL, d_ff), lambda i, e, c: (e, 0, 1)),
                  pl.BlockSpec((1, 1, d_ff), lambda i, e, c: (e, 0, 0)),
                  pl.BlockSpec((1, 1, d_ff), lambda i, e, c: (e, 0, 0)),
                  pl.BlockSpec((1, d_ff, D_MODEL), lambda i, e, c: (e, 0, 0)),
                  pl.BlockSpec((1, 1, D_MODEL), lambda i, e, c: (e, 0, 0))],
        out_specs=row(D_MODEL),
        scratch_shapes=[pltpu.VMEM((tm, N_EXPERTS), F32), pltpu.VMEM((N_EXPERTS, tm), F32)],
    )
    mixed = pl.pallas_call(
        functools.partial(_moe_kernel, tm=tm),
        grid_spec=grid_spec,
        out_shape=jax.ShapeDtypeStruct((nt * tm, D_MODEL), F32),
        compiler_params=_cparams(("parallel", "arbitrary")),
        name="moe_experts",
    )(counts, x16, gate, gate_t, w1p, w1p, b1g, b1l, w2, b2)
    return _residual_ln(x1, mixed, lg, lb, tm_rows, alpha)


def _deinterleave_w1(w1):
    f2 = w1.shape[-1]
    f = lax.broadcasted_iota(I32, (f2, f2), 0)
    g = lax.broadcasted_iota(I32, (f2, f2), 1)
    src = jnp.where(g < f2 // 2, 2 * g, 2 * (g - f2 // 2) + 1)
    perm = (f == src).astype(BF16)
    return jnp.einsum("edf,fg->edg", w1.astype(BF16), perm, preferred_element_type=BF16)


def _repack_w_in(w, b):
    offs = np.concatenate([[0], np.cumsum(PROJ_SIZES)])
    seg = lambda a, i: a[..., int(offs[i]):int(offs[i + 1])]
    d = w.shape[0]

    def build(a, rows):
        z = lambda n: jnp.zeros(rows + (n,), a.dtype)
        misc = jnp.concatenate([z(MISC_BF), seg(a, 6), seg(a, 12), z(LANES - MISC_IW - H_I)], axis=-1)
        main = jnp.concatenate([seg(a, 0), seg(a, 1), seg(a, 2), seg(a, 3), seg(a, 4), seg(a, 5),
                                seg(a, 7), seg(a, 8), seg(a, 9), seg(a, 10), seg(a, 11), seg(a, 11), misc], axis=-1)
        return main, seg(a, 13)

    wm, wg = build(w, (d,))
    bm, bg = build(b[None, :], (1,))
    return wm.astype(BF16), bm, wg.astype(BF16), bg


def _rope_tables(pos):
    half = ROT_DIM // 2
    inv = ROPE_THETA ** (-jnp.arange(half, dtype=F32) / half)
    ang = pos.astype(F32)[:, None] * inv
    cos, sin = jnp.cos(ang), jnp.sin(ang)
    m = pos.shape[0]
    ones = jnp.ones((m, DH - ROT_DIM), F32)
    zeros = jnp.zeros((m, DH - ROT_DIM), F32)
    z8 = jnp.zeros((m, half), F32)
    c64 = jnp.concatenate([cos, cos, ones], axis=1)
    a64 = jnp.concatenate([-sin, z8, zeros], axis=1)
    b64 = jnp.concatenate([z8, sin, zeros], axis=1)
    dup = lambda a: jnp.concatenate([a, a], axis=1)
    return dup(c64), dup(a64), dup(b64)


def kernel(x_prompt, x_sample, cache_a_k, cache_a_v, cache_b_k, cache_b_v, cache_b_logf, cache_c_k, cache_c_v,
           cache_c_idx_k, page_table, meta, w_in, b_in, a_lambda, a_norm_g, w_br_a, w_br_b, w_br_c, w_out,
           ln1_g, ln1_b, ln2_g, ln2_b, w_router, b_router, w_mlp1, b_mlp1, w_mlp2, b_mlp2):
    n_p, seq, _ = x_prompt.shape
    nseq, dec, _ = x_sample.shape
    depth = w_in.shape[0]
    pages = page_table.shape[1]
    past_len = pages * PAGE
    t = seq + BLOCK_Q
    topk_prompt = min(TOPK_MAX, seq // 4)
    topk_sample = min(TOPK_MAX, (past_len + dec) // 4)
    alpha = (2 * depth) ** 0.25
    mp = n_p * t
    m_all = mp + nseq * dec
    tm = _pick(m_all, (768, 640, 512, 384, 256, 128, 64, 32, 16, 8))

    xp = jnp.concatenate([jnp.zeros((n_p, META_PAD, D_MODEL), F32),
                          jnp.broadcast_to(meta[None], (n_p, N_META, D_MODEL)), x_prompt], axis=1)
    x = jnp.concatenate([xp.reshape(mp, D_MODEL), x_sample.reshape(nseq * dec, D_MODEL)], axis=0)
    pos = jnp.concatenate([jnp.tile(jnp.arange(t, dtype=I32) - META_PAD, n_p),
                           jnp.tile(past_len + jnp.arange(dec, dtype=I32), nseq)])
    cos, sa, sb = _rope_tables(pos)
    pt_flat = page_table.reshape(-1).astype(I32)
    flat_a = lambda c: c.reshape(c.shape[0], c.shape[1], PAGE * H_A, 2 * DH)
    key_minor = lambda c: jnp.transpose(c, (0, 1, 3, 4, 2))
    caches = (flat_a(cache_a_k), flat_a(cache_a_v), key_minor(cache_b_k), key_minor(cache_b_v),
              key_minor(cache_c_k), key_minor(cache_c_v))
    idx_kt = jnp.transpose(cache_c_idx_k, (0, 1, 3, 2))
    logf_t_cache = jnp.transpose(cache_b_logf, (0, 1, 3, 2))

    rows_p, rows_s = [], []
    for layer in range(depth):
        wm, bm, wg, bg = _repack_w_in(w_in[layer], b_in[layer])
        x16 = x.astype(BF16)
        p32, p16 = _project_main(x16, wm, bm, cos, sa, sb, tm)
        gates = _project_gates(x16, wg, bg, tm)
        lam = a_lambda[layer]
        g = a_norm_g[layer][None, :]

        bf_t = jnp.swapaxes(p32[:mp, MISC * LANES + MISC_BF:MISC * LANES + MISC_BF + H_B].reshape(n_p, t, H_B), 1, 2)
        logf_t, csum_t = _prompt_cumsum(bf_t)
        bias = _prompt_index(p16, p32, n_p, t, topk_prompt)
        oa = _prompt_flash("A", p16, (lam, g), n_p, t, layer)
        ob = _prompt_flash("B", p16, (jnp.swapaxes(csum_t, 1, 2), csum_t), n_p, t, layer)
        oc = _prompt_flash("C", p16, (bias,), n_p, t, layer)

        p32s = p32[mp:]
        sbias, scsum, scq, slogf = _sample_pre(pt_flat, p32s, idx_kt, logf_t_cache, layer, nseq, dec,
                                               pages, topk_sample)
        soa, sob, soc = _sample_attn(pt_flat, p32s, caches, scsum, scq, sbias, lam, g, layer, nseq, dec, pages)

        oa = jnp.concatenate([oa, soa.astype(BF16)], axis=0)
        ob = jnp.concatenate([ob, sob.astype(BF16)], axis=0)
        oc = jnp.concatenate([oc, soc.astype(BF16)], axis=0)
        x1, x1_16, gate = _merge(oa, ob, oc, gates, x,
                                 w_br_a[layer].astype(BF16), w_br_b[layer].astype(BF16), w_br_c[layer].astype(BF16),
                                 w_out[layer].astype(BF16), ln1_g[layer][None], ln1_b[layer][None],
                                 w_router[layer], b_router[layer][None], tm, alpha)
        x = _moe(x1_16, x1, gate, _deinterleave_w1(w_mlp1[layer]),
                 b_mlp1[layer][:, None, 0::2], b_mlp1[layer][:, None, 1::2],
                 w_mlp2[layer].astype(BF16), b_mlp2[layer][:, None, :],
                 ln2_g[layer][None], ln2_b[layer][None], tm, alpha)

        def seg(rows, col, width, shape):
            return rows[:, col * LANES:col * LANES + width].reshape(shape)

        tp = t - META_PAD
        p32p = p32[:mp].reshape(n_p, t, N_MAIN)

        def pseg(col, width, shape):
            return p32p[:, META_PAD:, col * LANES:col * LANES + width].reshape(shape)

        rows_p.append((pseg(AK, W_A, (n_p, tp, H_A, 2 * DH)), pseg(AV, W_A, (n_p, tp, H_A, 2 * DH)),
                       pseg(BK, W_B, (n_p, tp, H_B, DH)), pseg(BV, W_B, (n_p, tp, H_B, DH)),
                       jnp.swapaxes(logf_t, 1, 2)[:, META_PAD:],
                       pseg(CK, W_C, (n_p, tp, H_C, DH)), pseg(CV, W_C, (n_p, tp, H_C, DH)),
                       pseg(IK, D_IDX, (n_p, tp, D_IDX))))
        rows_s.append((seg(p32s, AK, W_A, (nseq, dec, H_A, 2 * DH)), seg(p32s, AV, W_A, (nseq, dec, H_A, 2 * DH)),
                       seg(p32s, BK, W_B, (nseq, dec, H_B, DH)), seg(p32s, BV, W_B, (nseq, dec, H_B, DH)),
                       slogf,
                       seg(p32s, CK, W_C, (nseq, dec, H_C, DH)), seg(p32s, CV, W_C, (nseq, dec, H_C, DH)),
                       seg(p32s, IK, D_IDX, (nseq, dec, D_IDX))))

    st = lambda rows, i: jnp.stack([r[i] for r in rows])
    y_prompt = x[:mp].reshape(n_p, t, D_MODEL)[:, BLOCK_Q:]
    y_sample = x[mp:].reshape(nseq, dec, D_MODEL)
    return (y_prompt, y_sample) + tuple(st(rows_p, i) for i in range(8)) + tuple(st(rows_s, i) for i in range(8))
```

```python
import functools
import math

import numpy as np
import jax
import jax.numpy as jnp
from jax import lax
from jax.experimental import pallas as pl
from jax.experimental.pallas import tpu as pltpu

F32 = jnp.float32
BF16 = jnp.bfloat16
I32 = jnp.int32

D_MODEL = 1024
DH = 64
H_A = 4
H_B = 8
H_C = 8
H_I = 4
D_IDX = 64
TOPK_MAX = 256
N_META = 16
BLOCK_Q = 128
META_PAD = BLOCK_Q - N_META
ROT_DIM = DH // 4
ROPE_THETA = 500000.0
N_EXPERTS = 32
TOP_K_EXPERTS = 4
SWIGLU_ALPHA = 1.702
SWIGLU_LIMIT = 7.0
LN_EPS = 1e-5
RMS_EPS = 1e-5
NEG = -1e30
M_INIT = float(np.finfo(np.float32).min)
PAGE = 128
LANES = 128
W_A = H_A * 2 * DH
W_B = H_B * DH
W_C = H_C * DH
PROJ_SIZES = (W_A, W_A, W_A, W_B, W_B, W_B, H_B, W_C, W_C, W_C, H_I * D_IDX, D_IDX, H_I, 3 * D_MODEL)
SCALE = DH ** -0.5
IW_SCALE = H_I ** -0.5 * D_IDX ** -0.5

AQ, AK, AV, BQ, BK, BV, CQ, CK, CV, IQ, IK, MISC = 0, 4, 8, 12, 16, 20, 24, 28, 32, 36, 38, 39
N_MAIN = 40 * LANES
MISC_BF = 16
MISC_IW = 24
PROJ_TN = 512
ROPE_TILES = (1, 1, 0, 0, 0, 0, 1, 1, 0, 1)
VMEM_LIMIT = 56 * 1024 * 1024
INT_MIN = -2 ** 31
FLASH_NB = 2

NT = (((1,), (1,)), ((), ()))


def _pick(n, cands):
    for c in cands:
        if n % c == 0:
            return c
    raise ValueError(f"no tile for {n}")


def _cparams(sem):
    return pltpu.CompilerParams(dimension_semantics=sem, vmem_limit_bytes=VMEM_LIMIT)


def _log_sigmoid(x):
    return jnp.minimum(x, 0.0) - jnp.log1p(jnp.exp(-jnp.abs(x)))


def _lane_scan(x):
    lane = lax.broadcasted_iota(I32, x.shape, 1)
    s = 1
    while s < LANES:
        x = x + jnp.where(lane >= s, pltpu.roll(x, s, 1), 0.0)
        s *= 2
    return x


def _rope_tile(x, cos, sa, sb):
    outs = []
    for c in range(x.shape[1] // LANES):
        xc = x[:, c * LANES:(c + 1) * LANES]
        up = pltpu.roll(xc, LANES - ROT_DIM // 2, 1)
        dn = pltpu.roll(xc, ROT_DIM // 2, 1)
        outs.append(xc * cos + up * sa + dn * sb)
    return jnp.concatenate(outs, axis=1)


def _proj_kernel(flags_ref, x_ref, w_ref, b_ref, cos_ref, sa_ref, sb_ref, o32_ref, o16_ref):
    j = pl.program_id(1)
    acc = jnp.dot(x_ref[...], w_ref[...], preferred_element_type=F32) + b_ref[...]

    @pl.when(flags_ref[j] == 0)
    def _():
        o32_ref[...] = acc
        o16_ref[...] = acc.astype(BF16)

    @pl.when(flags_ref[j] == 1)
    def _():
        r = _rope_tile(acc, cos_ref[...], sa_ref[...], sb_ref[...])
        o32_ref[...] = r
        o16_ref[...] = r.astype(BF16)


def _project_main(x16, w, b, cos, sa, sb, tm):
    m = x16.shape[0]
    n = w.shape[1]
    flags = jnp.asarray(ROPE_TILES, I32)
    grid_spec = pltpu.PrefetchScalarGridSpec(
        num_scalar_prefetch=1,
        grid=(m // tm, n // PROJ_TN),
        in_specs=[
            pl.BlockSpec((tm, D_MODEL), lambda i, j, f: (i, 0)),
            pl.BlockSpec((D_MODEL, PROJ_TN), lambda i, j, f: (0, j)),
            pl.BlockSpec((1, PROJ_TN), lambda i, j, f: (0, j)),
            pl.BlockSpec((tm, LANES), lambda i, j, f: (i, 0)),
            pl.BlockSpec((tm, LANES), lambda i, j, f: (i, 0)),
            pl.BlockSpec((tm, LANES), lambda i, j, f: (i, 0)),
        ],
        out_specs=[
            pl.BlockSpec((tm, PROJ_TN), lambda i, j, f: (i, j)),
            pl.BlockSpec((tm, PROJ_TN), lambda i, j, f: (i, j)),
        ],
    )
    return pl.pallas_call(
        _proj_kernel,
        grid_spec=grid_spec,
        out_shape=[jax.ShapeDtypeStruct((m, n), F32), jax.ShapeDtypeStruct((m, n), BF16)],
        compiler_params=_cparams(("parallel", "arbitrary")),
        name="proj_main",
    )(flags, x16, w, b, cos, sa, sb)


def _gate_kernel(x_ref, w_ref, b_ref, o_ref):
    acc = jnp.dot(x_ref[...], w_ref[...], preferred_element_type=F32) + b_ref[...]
    o_ref[...] = jax.nn.sigmoid(acc)


def _project_gates(x16, w, b, tm):
    m = x16.shape[0]
    n = w.shape[1]
    return pl.pallas_call(
        _gate_kernel,
        grid=(m // tm, n // PROJ_TN),
        in_specs=[
            pl.BlockSpec((tm, D_MODEL), lambda i, j: (i, 0)),
            pl.BlockSpec((D_MODEL, PROJ_TN), lambda i, j: (0, j)),
            pl.BlockSpec((1, PROJ_TN), lambda i, j: (0, j)),
        ],
        out_specs=pl.BlockSpec((tm, PROJ_TN), lambda i, j: (i, j)),
        out_shape=jax.ShapeDtypeStruct((m, n), F32),
        compiler_params=_cparams(("parallel", "arbitrary")),
        name="proj_gates",
    )(x16, w, b)


def _cumsum_kernel(bf_ref, logf_ref, csum_ref, *, t):
    def body(c, carry):
        sl = pl.ds(pl.multiple_of(c * LANES, LANES), LANES)
        tok = c * LANES + lax.broadcasted_iota(I32, (H_B, LANES), 1)
        lf = jnp.where(tok >= META_PAD, _log_sigmoid(bf_ref[0, :, sl]), 0.0)
        logf_ref[0, :, sl] = lf
        cs = _lane_scan(lf) + carry
        csum_ref[0, :, sl] = cs
        return cs[:, LANES - 1:LANES]

    lax.fori_loop(0, t // LANES, body, jnp.zeros((H_B, 1), F32))


def _prompt_cumsum(bf_t):
    n, h, t = bf_t.shape
    return pl.pallas_call(
        functools.partial(_cumsum_kernel, t=t),
        grid=(n,),
        in_specs=[pl.BlockSpec((1, h, t), lambda b: (b, 0, 0))],
        out_specs=[pl.BlockSpec((1, h, t), lambda b: (b, 0, 0)),
                   pl.BlockSpec((1, h, t), lambda b: (b, 0, 0))],
        out_shape=[jax.ShapeDtypeStruct((n, h, t), F32), jax.ShapeDtypeStruct((n, h, t), F32)],
        compiler_params=_cparams(("parallel",)),
        name="prompt_cumsum",
    )(bf_t)


def _topk_select(sc_ref, key_ref, out_ref, nsuper, cw, k, rows, out_dtype):
    def sl_of(c):
        return pl.ds(pl.multiple_of(c * LANES, LANES), LANES)

    def make_keys(sc, _):
        for jj in range(cw):
            sl = sl_of(sc * cw + jj)
            bits = pltpu.bitcast(sc_ref[:, sl], I32)
            key_ref[:, sl] = jnp.where(bits < 0, bits ^ jnp.int32(0x7FFFFFFF), bits)
        return 0

    lax.fori_loop(0, nsuper, make_keys, 0)

    def count(pred):
        def body(sc, acc):
            for jj in range(cw):
                acc = acc + jnp.where(pred(key_ref[:, sl_of(sc * cw + jj)]), 1, 0).astype(I32)
            return acc
        acc = lax.fori_loop(0, nsuper, body, jnp.zeros((rows, LANES), I32))
        return jnp.sum(acc, axis=1, keepdims=True)

    c0 = count(lambda key: key >= 0)
    t0 = jnp.where(c0 >= k, jnp.int32(0), jnp.int32(INT_MIN))

    def bit_body(i, t):
        cand = t | jnp.left_shift(jnp.int32(1), 30 - i)
        cnt = count(lambda key: key >= cand)
        return jnp.where(cnt >= k, cand, t)

    thr = lax.fori_loop(0, 31, bit_body, t0)
    need = (k - count(lambda key: key > thr)).astype(F32)
    upper = (lax.broadcasted_iota(I32, (LANES, LANES), 0)
             < lax.broadcasted_iota(I32, (LANES, LANES), 1)).astype(BF16)

    def sel_body(sc, carry):
        for jj in range(cw):
            sl = sl_of(sc * cw + jj)
            key = key_ref[:, sl]
            eq = key == thr
            eqf = jnp.where(eq, 1.0, 0.0)
            rank = jnp.dot(eqf.astype(BF16), upper, preferred_element_type=F32) + carry
            sel = (key > thr) | (eq & (rank < need))
            valid = sc_ref[:, sl] > 0.5 * NEG
            out_ref[:, sl] = jnp.where(sel & valid, 0.0, NEG).astype(out_dtype)
            carry = carry + jnp.sum(eqf, axis=1, keepdims=True)
        return carry

    lax.fori_loop(0, nsuper, sel_body, jnp.zeros((rows, 1), F32))


def _prompt_index_kernel(iq_ref, ik_ref, misc_ref, bias_ref, sc_scr, key_scr, *, k, cw):
    i = pl.program_id(1)
    wide = cw * LANES
    nsuper = (i + cw) // cw
    lo = lax.broadcasted_iota(I32, (1, LANES), 1) < DH
    zero = jnp.zeros((BLOCK_Q, LANES), BF16)
    iq_heads = []
    for pair in range(H_I // 2):
        blk = iq_ref[:, pair * LANES:(pair + 1) * LANES]
        iq_heads.append(jnp.where(lo, blk, zero))
        iq_heads.append(jnp.where(lo, zero, blk))
    iw = misc_ref[...][:, MISC_IW:MISC_IW + H_I] * IW_SCALE
    iw_wide = [jnp.broadcast_to(iw[:, h:h + 1], (BLOCK_Q, wide)) for h in range(H_I)]
    qidx = i * BLOCK_Q + lax.broadcasted_iota(I32, (BLOCK_Q, 1), 0)

    def chunk(c, _):
        sl = pl.ds(pl.multiple_of(c * wide, wide), wide)
        kmat = ik_ref[sl, :]
        acc = None
        for h in range(H_I):
            s = lax.dot_general(iq_heads[h], kmat, NT, preferred_element_type=F32)
            term = jnp.maximum(s, 0.0) * iw_wide[h]
            acc = term if acc is None else acc + term
        kidx = c * wide + lax.broadcasted_iota(I32, (1, wide), 1)
        ok = (kidx <= qidx) & (kidx >= META_PAD)
        sc_scr[:, sl] = jnp.where(ok, acc, NEG)
        return 0

    lax.fori_loop(0, nsuper, chunk, 0)
    bias_ref[...] = jnp.full(bias_ref.shape, NEG, bias_ref.dtype)
    _topk_select(sc_scr, key_scr, bias_ref.at[0], nsuper, cw, k, BLOCK_Q, bias_ref.dtype)


def _prompt_index(p16, p32, n, t, k):
    nq = t // BLOCK_Q
    return pl.pallas_call(
        functools.partial(_prompt_index_kernel, k=k, cw=_pick(t, (640, 512, 384, 256, 128)) // LANES),
        grid=(n, nq),
        in_specs=[
            pl.BlockSpec((BLOCK_Q, 2 * LANES), lambda b, i: (b * nq + i, IQ // 2)),
            pl.BlockSpec((t, LANES), lambda b, i: (b, IK)),
            pl.BlockSpec((BLOCK_Q, LANES), lambda b, i: (b * nq + i, MISC)),
        ],
        out_specs=pl.BlockSpec((1, BLOCK_Q, t), lambda b, i: (b, i, 0)),
        out_shape=jax.ShapeDtypeStruct((n, t, t), BF16),
        scratch_shapes=[pltpu.VMEM((BLOCK_Q, t), F32), pltpu.VMEM((BLOCK_Q, t), I32)],
        compiler_params=_cparams(("parallel", "arbitrary")),
        name="prompt_index_topk",
    )(p16, p16, p32)


def _diff_lambda(lam_ref, layer):
    lam_init = 0.8 - 0.6 * math.exp(-0.3 * layer)
    lv = lam_ref[...]
    a = jnp.sum(lv[0:1] * lv[1:2], axis=1, keepdims=True)
    b = jnp.sum(lv[2:3] * lv[3:4], axis=1, keepdims=True)
    return jnp.exp(a) - jnp.exp(b) + lam_init, lam_init


def _diff_finish(o0, o1, lam, lam_init, g):
    o = o0 - lam * o1
    o = o * lax.rsqrt(jnp.mean(jnp.square(o), axis=-1, keepdims=True) + RMS_EPS)
    return o * g * (1.0 - lam_init)


def _flash_kernel(qi_ref, kj_ref, *refs, mode, tq, tk, layer):
    if mode == "A":
        q_ref, k_ref, v_ref, lam_ref, g_ref, o_ref, m_scr, acc_scr = refs
    elif mode == "B":
        q_ref, k_ref, v_ref, cq_ref, ck_ref, o_ref, m_scr, acc_scr, cq_scr = refs
    else:
        q_ref, k_ref, v_ref, bias_ref, o_ref, m_scr, acc_scr = refs
    cb = pl.program_id(1)
    step = pl.program_id(2)
    i = qi_ref[step]
    j = kj_ref[step]
    lo = lax.broadcasted_iota(I32, (1, LANES), 1) < DH

    @pl.when(j == 0)
    def _():
        m_scr[...] = jnp.full(m_scr.shape, M_INIT, F32)
        acc_scr[...] = jnp.zeros(acc_scr.shape, F32)
        if mode == "B":
            lane8 = lax.broadcasted_iota(I32, (1, H_B), 1)
            for u in range(2 * FLASH_NB):
                cq_scr[u] = jnp.sum(jnp.where(lane8 == 2 * FLASH_NB * cb + u, cq_ref[0], 0.0),
                                    axis=1, keepdims=True)

    def update(masked):
        q_all = q_ref[...] * jnp.asarray(SCALE, BF16)
        k_all = k_ref[...]
        v_all = v_ref[...]
        zero = jnp.zeros((tq, LANES), BF16)
        ones = jnp.ones((tk, LANES), BF16)
        if masked:
            qidx = i * tq + lax.broadcasted_iota(I32, (tq, 1), 0)
            kidx = j * tk + lax.broadcasted_iota(I32, (1, tk), 1)
            ok = (kidx <= qidx) & (kidx >= META_PAD)
        for u in range(2 * FLASH_NB):
            cs = slice((u // 2) * LANES, (u // 2 + 1) * LANES)
            q = q_all[:, cs]
            k = k_all[:, cs]
            v_ext = jnp.concatenate([v_all[:, cs], ones], axis=1)
            qu = jnp.where(lo, q, zero) if u % 2 == 0 else jnp.where(lo, zero, q)
            s = lax.dot_general(qu, k, NT, preferred_element_type=F32)
            if mode == "B":
                s = s + cq_scr[u] - ck_ref[0, pl.ds(2 * FLASH_NB * cb + u, 1), :]
            if mode == "C":
                s = s + bias_ref[0].astype(F32)
            if masked:
                s = jnp.where(ok, s, NEG)
            m_prev = m_scr[u]
            m_new = jnp.maximum(m_prev, jnp.max(s, axis=1, keepdims=True))
            alpha = jnp.exp(m_prev - m_new)
            p = jnp.exp((s - m_new).astype(BF16))
            acc_scr[u] = alpha * acc_scr[u] + jnp.dot(p, v_ext, preferred_element_type=F32)
            m_scr[u] = m_new

    if mode == "C":
        update(False)
    else:
        edge = (j == i) | (j == 0)

        @pl.when(edge)
        def _():
            update(True)

        @pl.when(jnp.logical_not(edge))
        def _():
            update(False)

    @pl.when(j == i)
    def _():
        outs = []
        for blk in range(FLASH_NB):
            o0 = acc_scr[2 * blk, :, 0:LANES] / acc_scr[2 * blk, :, LANES:LANES + 1]
            o1 = acc_scr[2 * blk + 1, :, 0:LANES] / acc_scr[2 * blk + 1, :, LANES:LANES + 1]
            if mode == "A":
                lam, lam_init = _diff_lambda(lam_ref, layer)
                outs.append(_diff_finish(o0, o1, lam, lam_init, g_ref[...]))
            else:
                outs.append(jnp.where(lo, o0, o1))
        o_ref[...] = jnp.concatenate(outs, axis=1).astype(o_ref.dtype)


def _prompt_flash(mode, p16, extras, n, t, layer):
    tq = _pick(t, (640, 512, 384, 256, 128))
    tk = tq
    nq = t // tq
    qi = np.array([i for i in range(nq) for _ in range(i + 1)], np.int32)
    kj = np.array([j for i in range(nq) for j in range(i + 1)], np.int32)
    qcol, kcol, vcol = {"A": (AQ, AK, AV), "B": (BQ, BK, BV), "C": (CQ, CK, CV)}[mode]
    nb = FLASH_NB
    wb = nb * LANES
    in_specs = [
        pl.BlockSpec((tq, wb), lambda b, c, s, qi, kj: (b * nq + qi[s], qcol // nb + c)),
        pl.BlockSpec((tk, wb), lambda b, c, s, qi, kj: (b * nq + kj[s], kcol // nb + c)),
        pl.BlockSpec((tk, wb), lambda b, c, s, qi, kj: (b * nq + kj[s], vcol // nb + c)),
    ]
    scratch = [pltpu.VMEM((2 * nb, tq, 1), F32), pltpu.VMEM((2 * nb, tq, 2 * LANES), F32)]
    if mode == "A":
        in_specs += [pl.BlockSpec((4, DH), lambda b, c, s, qi, kj: (0, 0)),
                     pl.BlockSpec((1, 2 * DH), lambda b, c, s, qi, kj: (0, 0))]
    elif mode == "B":
        in_specs += [pl.BlockSpec((1, tq, H_B), lambda b, c, s, qi, kj: (b, qi[s], 0)),
                     pl.BlockSpec((1, H_B, tk), lambda b, c, s, qi, kj: (b, 0, kj[s]))]
        scratch += [pltpu.VMEM((2 * nb, tq, 1), F32)]
    else:
        in_specs += [pl.BlockSpec((1, tq, tk), lambda b, c, s, qi, kj: (b, qi[s], kj[s]))]
    grid_spec = pltpu.PrefetchScalarGridSpec(
        num_scalar_prefetch=2,
        grid=(n, 4 // nb, len(qi)),
        in_specs=in_specs,
        out_specs=pl.BlockSpec((tq, wb), lambda b, c, s, qi, kj: (b * nq + qi[s], c)),
        scratch_shapes=scratch,
    )
    return pl.pallas_call(
        functools.partial(_flash_kernel, mode=mode, tq=tq, tk=tk, layer=layer),
        grid_spec=grid_spec,
        out_shape=jax.ShapeDtypeStruct((n * t, 4 * LANES), BF16),
        compiler_params=_cparams(("parallel", "parallel", "arbitrary")),
        name=f"prompt_flash_{mode}",
    )(jnp.asarray(qi), jnp.asarray(kj), p16, p16, p16, *extras)


def _pad_rows(x, rows):
    return jnp.concatenate([x, jnp.zeros((rows - x.shape[0], x.shape[1]), x.dtype)], axis=0)


def _cols(ref, start, width):
    blk = start // LANES
    off = start - blk * LANES
    x = ref[:, blk * LANES:(blk + 1) * LANES]
    return x if width == LANES else x[:, off:off + width]


def _stack_rows(x, reps):
    return jnp.concatenate([x] * reps, axis=0)


def _spread_rows(x, reps):
    return jnp.concatenate([jnp.broadcast_to(x[h:h + 1], (reps, x.shape[1])) for h in range(x.shape[0])], axis=0)


def _sample_pre_kernel(pt_ref, proj_ref, *refs, pages, group, k, dec):
    ikt_refs, lft_refs = refs[:group], refs[group:2 * group]
    bias_ref, csum_ref, cq_ref, lnew_ref, sc_scr, key_scr, carry_scr = refs[2 * group:]
    p = pl.program_id(1)
    nsteps = pages // group
    span = group * LANES
    iq = jnp.concatenate([_cols(proj_ref, IQ * LANES + h * D_IDX, D_IDX) for h in range(H_I)], axis=0).astype(BF16)
    iw = _cols(proj_ref, MISC * LANES + MISC_IW, H_I) * IW_SCALE
    iw_rows = jnp.concatenate([iw[:, h:h + 1] for h in range(H_I)], axis=0)

    def scores(s):
        w = jnp.maximum(s, 0.0) * iw_rows
        out = w[0:dec]
        for h in range(1, H_I):
            out = out + w[h * dec:(h + 1) * dec]
        return out

    @pl.when(p == 0)
    def _():
        carry_scr[...] = jnp.zeros(carry_scr.shape, F32)

    @pl.when(p < nsteps)
    def _():
        kt = jnp.concatenate([r[0, 0] for r in ikt_refs], axis=1).astype(BF16)
        sc_scr[:, pl.ds(pl.multiple_of(p * span, span), span)] = scores(
            jnp.dot(iq, kt, preferred_element_type=F32))
        scans = [_lane_scan(r[0, 0]) for r in lft_refs]
        carry = carry_scr[...]
        for g, sc in enumerate(scans):
            csum_ref[0, :, pl.ds(pl.multiple_of(p * span + g * LANES, LANES), LANES)] = sc + carry
            carry = carry + sc[:, LANES - 1:LANES]
        carry_scr[...] = carry

    @pl.when(p == nsteps)
    def _():
        tail = slice(pages * LANES, (pages + 1) * LANES)
        knew = _pad_rows(_cols(proj_ref, IK * LANES, D_IDX).astype(BF16), LANES)
        s = scores(lax.dot_general(iq, knew, NT, preferred_element_type=F32))
        row = lax.broadcasted_iota(I32, (dec, LANES), 0)
        lane = lax.broadcasted_iota(I32, (dec, LANES), 1)
        sc_scr[:, tail] = jnp.where(lane <= row, s, NEG)
        lnew = _log_sigmoid(_cols(proj_ref, MISC * LANES + MISC_BF, H_B))
        lnew_ref[0] = lnew
        eye = (lax.broadcasted_iota(I32, (H_B, H_B), 0) == lax.broadcasted_iota(I32, (H_B, H_B), 1)).astype(F32)
        lt = lax.dot_general(eye, _pad_rows(lnew, LANES), NT, precision=lax.Precision.HIGHEST,
                             preferred_element_type=F32)
        cs = _lane_scan(lt) + carry_scr[...]
        csum_ref[0, :, tail] = cs
        pick = (row == lane).astype(F32)
        cq_ref[0] = lax.dot_general(pick, cs, NT, precision=lax.Precision.HIGHEST, preferred_element_type=F32)
        _topk_select(sc_scr, key_scr, bias_ref.at[0], 1, pages + 1, k, dec, F32)


def _page_map(layer, pages, group, g, ndim):
    nsteps = pages // group

    def index_map(b, p, pt):
        return (layer, pt[b * pages + jnp.minimum(p, nsteps - 1) * group + g]) + (0,) * (ndim - 2)
    return index_map


def _sample_pre(pt_flat, p32s, ikt, lft, layer, nseq, dec, pages, k):
    width = (pages + 1) * LANES
    group = _pick(pages, (8, 4, 2, 1))
    in_specs = [pl.BlockSpec((dec, N_MAIN), lambda b, p, pt: (b, 0))]
    in_specs += [pl.BlockSpec((1, 1, D_IDX, PAGE), _page_map(layer, pages, group, g, 4)) for g in range(group)]
    in_specs += [pl.BlockSpec((1, 1, H_B, PAGE), _page_map(layer, pages, group, g, 4)) for g in range(group)]
    grid_spec = pltpu.PrefetchScalarGridSpec(
        num_scalar_prefetch=1,
        grid=(nseq, pages // group + 1),
        in_specs=in_specs,
        out_specs=[
            pl.BlockSpec((1, dec, width), lambda b, p, pt: (b, 0, 0)),
            pl.BlockSpec((1, H_B, width), lambda b, p, pt: (b, 0, 0)),
            pl.BlockSpec((1, dec, H_B), lambda b, p, pt: (b, 0, 0)),
            pl.BlockSpec((1, dec, H_B), lambda b, p, pt: (b, 0, 0)),
        ],
        scratch_shapes=[pltpu.VMEM((dec, width), F32), pltpu.VMEM((dec, width), I32), pltpu.VMEM((H_B, 1), F32)],
    )
    return pl.pallas_call(
        functools.partial(_sample_pre_kernel, pages=pages, group=group, k=k, dec=dec),
        grid_spec=grid_spec,
        out_shape=[jax.ShapeDtypeStruct((nseq, dec, width), F32),
                   jax.ShapeDtypeStruct((nseq, H_B, width), F32),
                   jax.ShapeDtypeStruct((nseq, dec, H_B), F32),
                   jax.ShapeDtypeStruct((nseq, dec, H_B), F32)],
        compiler_params=_cparams(("parallel", "arbitrary")),
        name="sample_index_topk",
    )(pt_flat, p32s, *([ikt] * group), *([lft] * group))


def _sample_attn_kernel(pt_ref, proj_ref, *refs, pages, group, layer, dec):
    caches = [refs[c * group:(c + 1) * group] for c in range(6)]
    ka_refs, va_refs, kb_refs, vb_refs, kc_refs, vc_refs = caches
    (csum_ref, cq_ref, bias_ref, lam_ref, g_ref, oa_ref, ob_ref, oc_ref,
     qa_scr, qb_scr, qc_scr, cqb_scr, ma, la, acca, mb, lb, accb, mc, lc, accc) = refs[6 * group:]
    nsteps = pages // group
    span = group * LANES
    p = pl.program_id(1)
    wide = H_B * DH
    lo = lax.broadcasted_iota(I32, (1, LANES), 1) < DH
    colhead = lax.broadcasted_iota(I32, (1, wide), 1) // DH
    states = ((ma, la, acca), (mb, lb, accb), (mc, lc, accc))

    @pl.when(p == 0)
    def _():
        for m_ref, l_ref, acc_ref in states:
            m_ref[...] = jnp.full(m_ref.shape, M_INIT, F32)
            l_ref[...] = jnp.zeros(l_ref.shape, F32)
            acc_ref[...] = jnp.zeros(acc_ref.shape, F32)
        pieces = []
        for h in range(H_A):
            qh = _cols(proj_ref, (AQ + h) * LANES, LANES) * SCALE
            pieces += [jnp.where(lo, qh, 0.0), jnp.where(lo, 0.0, qh)]
        qa_scr[...] = jnp.concatenate(pieces, axis=0).astype(BF16)
        for q_scr, col in ((qb_scr, BQ), (qc_scr, CQ)):
            q_all = proj_ref[:, col * LANES:col * LANES + wide] * SCALE
            q_scr[...] = jnp.concatenate([jnp.where(colhead == h, q_all, 0.0) for h in range(H_B)],
                                         axis=0).astype(BF16)
        cq = cq_ref[0]
        cqb_scr[...] = jnp.concatenate([cq[:, h:h + 1] for h in range(H_B)], axis=0)

    def online(state, rows, s, pv):
        m_ref, l_ref, acc_ref = state
        m_prev = m_ref[rows, :]
        m_new = jnp.maximum(m_prev, jnp.max(s, axis=1, keepdims=True))
        alpha = jnp.exp(m_prev - m_new)
        pr = jnp.exp(s - m_new)
        l_ref[rows, :] = alpha * l_ref[rows, :] + jnp.sum(pr, axis=1, keepdims=True)
        acc_ref[rows, :] = alpha * acc_ref[rows, :] + pv(pr.astype(BF16))
        m_ref[rows, :] = m_new

    everything = slice(None)

    def b_bias(cols):
        return cqb_scr[...] - _spread_rows(csum_ref[0, :, cols], dec)

    def c_bias(cols):
        return _stack_rows(bias_ref[0, :, cols], H_C)

    @pl.when(p < nsteps)
    def _():
        cols = pl.ds(pl.multiple_of(p * span, span), span)
        ka = jnp.concatenate([r[0, 0] for r in ka_refs], axis=0).astype(BF16)
        va = jnp.concatenate([r[0, 0] for r in va_refs], axis=0).astype(BF16)
        sa = lax.dot_general(qa_scr[...], ka, NT, preferred_element_type=F32)
        rowh = lax.broadcasted_iota(I32, (2 * H_A * dec, 1), 0) // (2 * dec)
        colh = lax.broadcasted_iota(I32, (1, group * H_A * PAGE), 1) % H_A
        sa = jnp.where(colh == rowh, sa, NEG)
        online(states[0], everything, sa, lambda pr: jnp.dot(pr, va, preferred_element_type=F32))
        for state, q_scr, k_refs, v_refs, bias in ((states[1], qb_scr, kb_refs, vb_refs, b_bias),
                                                   (states[2], qc_scr, kc_refs, vc_refs, c_bias)):
            q = q_scr[...]
            s = jnp.concatenate([jnp.dot(q, r[0, 0].reshape(wide, PAGE).astype(BF16), preferred_element_type=F32)
                                 for r in k_refs], axis=1) + bias(cols)

            def pv(pr, v_refs=v_refs):
                out = None
                for g, r in enumerate(v_refs):
                    part = lax.dot_general(pr[:, g * LANES:(g + 1) * LANES], r[0, 0].reshape(wide, PAGE).astype(BF16),
                                           NT, preferred_element_type=F32)
                    out = part if out is None else out + part
                return out

            online(state, everything, s, pv)

    @pl.when(p == nsteps)
    def _():
        tail = slice(pages * LANES, (pages + 1) * LANES)

        def causal(nrows):
            qrow = lax.broadcasted_iota(I32, (nrows, LANES), 0) % dec
            return lax.broadcasted_iota(I32, (nrows, LANES), 1) <= qrow

        for h in range(H_A):
            rows = slice(2 * dec * h, 2 * dec * (h + 1))
            kn = _pad_rows(_cols(proj_ref, (AK + h) * LANES, LANES).astype(BF16), LANES)
            vn = _pad_rows(_cols(proj_ref, (AV + h) * LANES, LANES).astype(BF16), LANES)
            s = lax.dot_general(qa_scr[rows, :], kn, NT, preferred_element_type=F32)
            s = jnp.where(causal(2 * dec), s, NEG)
            online(states[0], rows, s, lambda pr, vn=vn: jnp.dot(pr, vn, preferred_element_type=F32))
        for state, q_scr, kcol, vcol, bias, masked in ((states[1], qb_scr, BK, BV, b_bias, True),
                                                       (states[2], qc_scr, CK, CV, c_bias, False)):
            kn = _pad_rows(proj_ref[:, kcol * LANES:kcol * LANES + wide].astype(BF16), LANES)
            vn = _pad_rows(proj_ref[:, vcol * LANES:vcol * LANES + wide].astype(BF16), LANES)
            s = lax.dot_general(q_scr[...], kn, NT, preferred_element_type=F32) + bias(tail)
            if masked:
                s = jnp.where(causal(H_B * dec), s, NEG)
            online(state, everything, s, lambda pr, vn=vn: jnp.dot(pr, vn, preferred_element_type=F32))

        lam, lam_init = _diff_lambda(lam_ref, layer)
        outs = []
        for h in range(H_A):
            r0 = slice(2 * dec * h, 2 * dec * h + dec)
            r1 = slice(2 * dec * h + dec, 2 * dec * (h + 1))
            outs.append(_diff_finish(acca[r0, :] / la[r0, :], acca[r1, :] / la[r1, :], lam, lam_init, g_ref[...]))
        oa_ref[...] = jnp.concatenate(outs, axis=1)
        for o_ref, l_ref, acc_ref in ((ob_ref, lb, accb), (oc_ref, lc, accc)):
            out = jnp.zeros((dec, wide), F32)
            for h in range(H_B):
                rows = slice(dec * h, dec * (h + 1))
                out = out + jnp.where(colhead == h, acc_ref[rows, :] / l_ref[rows, :], 0.0)
            o_ref[...] = out


def _sample_attn(pt_flat, p32s, caches, csum_t, cq, bias, lam, g, layer, nseq, dec, pages):
    width = (pages + 1) * LANES
    wide = H_B * DH
    group = _pick(pages, (4, 2, 1))
    seq_map = lambda b, p, pt: (b, 0, 0)
    in_specs = [pl.BlockSpec((dec, N_MAIN), lambda b, p, pt: (b, 0))]
    operands = []
    for c in caches:
        in_specs += [pl.BlockSpec((1, 1) + c.shape[2:], _page_map(layer, pages, group, gi, c.ndim))
                     for gi in range(group)]
        operands += [c] * group
    in_specs += [
        pl.BlockSpec((1, H_B, width), seq_map),
        pl.BlockSpec((1, dec, H_B), seq_map),
        pl.BlockSpec((1, dec, width), seq_map),
        pl.BlockSpec((4, DH), lambda b, p, pt: (0, 0)),
        pl.BlockSpec((1, 2 * DH), lambda b, p, pt: (0, 0)),
    ]
    rows = H_B * dec
    st = lambda w: [pltpu.VMEM((rows, 1), F32), pltpu.VMEM((rows, 1), F32), pltpu.VMEM((rows, w), F32)]
    grid_spec = pltpu.PrefetchScalarGridSpec(
        num_scalar_prefetch=1,
        grid=(nseq, pages // group + 1),
        in_specs=in_specs,
        out_specs=[pl.BlockSpec((dec, 4 * LANES), lambda b, p, pt: (b, 0))] * 3,
        scratch_shapes=[pltpu.VMEM((rows, LANES), BF16), pltpu.VMEM((rows, wide), BF16), pltpu.VMEM((rows, wide), BF16),
                        pltpu.VMEM((rows, 1), F32)] + st(2 * DH) + st(wide) + st(wide),
    )
    return pl.pallas_call(
        functools.partial(_sample_attn_kernel, pages=pages, group=group, layer=layer, dec=dec),
        grid_spec=grid_spec,
        out_shape=[jax.ShapeDtypeStruct((nseq * dec, 4 * LANES), F32)] * 3,
        compiler_params=_cparams(("parallel", "arbitrary")),
        name="sample_attention",
    )(pt_flat, p32s, *operands, csum_t, cq, bias, lam, g)


def _layer_norm(h, g, b):
    mu = jnp.mean(h, axis=-1, keepdims=True)
    var = jnp.mean(jnp.square(h - mu), axis=-1, keepdims=True)
    return (h - mu) * lax.rsqrt(var + LN_EPS) * g + b


def _merge_kernel(oa_ref, ob_ref, oc_ref, g_ref, x_ref, wa_ref, wb_ref, wc_ref, wo_ref, lg_ref, lb_ref,
                  wr_ref, br_ref, x1_ref, x16_ref, gate_ref, *, alpha):
    ya = jnp.dot(oa_ref[...], wa_ref[...], preferred_element_type=F32)
    yb = jnp.dot(ob_ref[...], wb_ref[...], preferred_element_type=F32)
    yc = jnp.dot(oc_ref[...], wc_ref[...], preferred_element_type=F32)
    merged = (g_ref[:, 0:D_MODEL] * ya + g_ref[:, D_MODEL:2 * D_MODEL] * yb
              + g_ref[:, 2 * D_MODEL:3 * D_MODEL] * yc)
    mix = jnp.dot(merged.astype(BF16), wo_ref[...], preferred_element_type=F32)
    x1 = _layer_norm(alpha * x_ref[...] + mix, lg_ref[...], lb_ref[...])
    x1_ref[...] = x1
    x16_ref[...] = x1.astype(BF16)
    logits = jnp.dot(x1, wr_ref[...], precision=lax.Precision.HIGHEST, preferred_element_type=F32) + br_ref[...]
    lane = lax.broadcasted_iota(I32, logits.shape, 1)
    work = logits
    vals, hots = [], []
    for _ in range(TOP_K_EXPERTS):
        mx = jnp.max(work, axis=1, keepdims=True)
        ix = jnp.min(jnp.where(work == mx, lane, N_EXPERTS), axis=1, keepdims=True)
        hot = lane == ix
        vals.append(mx)
        hots.append(hot)
        work = jnp.where(hot, -jnp.inf, work)
    es = [jnp.exp(v - vals[0]) for v in vals]
    den = es[0] + es[1] + es[2] + es[3]
    gate = jnp.zeros(logits.shape, F32)
    for e, hot in zip(es, hots):
        gate = gate + jnp.where(hot, e / den, 0.0)
    gate_ref[...] = gate


def _merge(oa, ob, oc, gates, x, wa, wb, wc, wo, lg, lb, wr, br, tm, alpha):
    m = x.shape[0]
    row = lambda w: pl.BlockSpec((tm, w), lambda i: (i, 0))
    full = lambda a: pl.BlockSpec(a.shape, lambda i: (0, 0))
    return pl.pallas_call(
        functools.partial(_merge_kernel, alpha=alpha),
        grid=(m // tm,),
        in_specs=[row(4 * LANES), row(4 * LANES), row(4 * LANES), row(3 * D_MODEL), row(D_MODEL),
                  full(wa), full(wb), full(wc), full(wo), full(lg), full(lb), full(wr), full(br)],
        out_specs=[row(D_MODEL), row(D_MODEL), row(N_EXPERTS)],
        out_shape=[jax.ShapeDtypeStruct((m, D_MODEL), F32), jax.ShapeDtypeStruct((m, D_MODEL), BF16),
                   jax.ShapeDtypeStruct((m, N_EXPERTS), F32)],
        compiler_params=_cparams(("parallel",)),
        name="merge_ln_router",
    )(oa, ob, oc, gates, x, wa, wb, wc, wo, lg, lb, wr, br)


MOE_CHUNK = 256
MOE_ROWS = 1536


def _moe_kernel(cnt_ref, x16_ref, gate_ref, gatet_ref, w1g_ref, w1l_ref, b1g_ref, b1l_ref, w2_ref, b2_ref,
                o_ref, rank_scr, rankt_scr, *, tm):
    i = pl.program_id(0)
    e = pl.program_id(1)

    @pl.when(e == 0)
    def _():
        o_ref[...] = jnp.zeros(o_ref.shape, F32)
        r = lax.broadcasted_iota(I32, (LANES, LANES), 0)
        c = lax.broadcasted_iota(I32, (LANES, LANES), 1)
        lower = (c < r).astype(BF16)
        upper = (r < c).astype(BF16)
        carry = jnp.zeros((1, N_EXPERTS), F32)
        carry_t = jnp.zeros((N_EXPERTS, 1), F32)
        for blk in range(tm // LANES):
            rows = slice(blk * LANES, (blk + 1) * LANES)
            sel = jnp.where(gate_ref[rows, :] > 0.0, 1.0, 0.0)
            selt = jnp.where(gatet_ref[:, rows] > 0.0, 1.0, 0.0)
            rank_scr[rows, :] = jnp.dot(lower, sel.astype(BF16), preferred_element_type=F32) + carry
            rankt_scr[:, rows] = jnp.dot(selt.astype(BF16), upper, preferred_element_type=F32) + carry_t
            carry = carry + jnp.sum(sel, axis=0, keepdims=True)
            carry_t = carry_t + jnp.sum(selt, axis=1, keepdims=True)

    hot = lax.broadcasted_iota(I32, (1, N_EXPERTS), 1) == e
    g_col = jnp.sum(jnp.where(hot, gate_ref[...], 0.0), axis=1, keepdims=True)
    r_col = jnp.sum(jnp.where(hot, rank_scr[...], 0.0), axis=1, keepdims=True)
    g_row = gatet_ref[pl.ds(e, 1), :]
    r_row = rankt_scr[pl.ds(e, 1), :]
    slot_r = lax.broadcasted_iota(I32, (MOE_CHUNK, 1), 0).astype(F32)
    slot_c = lax.broadcasted_iota(I32, (1, MOE_CHUNK), 1).astype(F32)
    nch = (cnt_ref[i * N_EXPERTS + e] + MOE_CHUNK - 1) // MOE_CHUNK

    def chunk(ci, _):
        base = (ci * MOE_CHUNK).astype(F32)
        pick = ((r_row - base) == slot_r) & (g_row > 0.0)
        pick_t = ((r_col - base) == slot_c) & (g_col > 0.0)
        xc = jnp.dot(jnp.where(pick, 1.0, 0.0).astype(BF16), x16_ref[...], preferred_element_type=F32).astype(BF16)
        gw = jnp.sum(jnp.where(pick, g_row, 0.0), axis=1, keepdims=True)
        glu = jnp.minimum(jnp.dot(xc, w1g_ref[0], preferred_element_type=F32) + b1g_ref[0], SWIGLU_LIMIT)
        lin = jnp.clip(jnp.dot(xc, w1l_ref[0], preferred_element_type=F32) + b1l_ref[0], -SWIGLU_LIMIT, SWIGLU_LIMIT)
        a = glu * jax.nn.sigmoid(SWIGLU_ALPHA * glu) * (lin + 1.0)
        y = jnp.dot(a.astype(BF16), w2_ref[0], preferred_element_type=F32) + b2_ref[0]
        o_ref[...] += jnp.dot(jnp.where(pick_t, 1.0, 0.0).astype(BF16), (y * gw).astype(BF16),
                              preferred_element_type=F32)
        return 0

    lax.fori_loop(0, nch, chunk, 0)


def _residual_ln_kernel(x_ref, y_ref, lg_ref, lb_ref, o_ref, *, alpha):
    o_ref[...] = _layer_norm(alpha * x_ref[...] + y_ref[...], lg_ref[...], lb_ref[...])


def _residual_ln(x, y, lg, lb, tm, alpha):
    m = x.shape[0]
    row = pl.BlockSpec((tm, D_MODEL), lambda i: (i, 0))
    vec = pl.BlockSpec((1, D_MODEL), lambda i: (0, 0))
    return pl.pallas_call(
        functools.partial(_residual_ln_kernel, alpha=alpha),
        grid=(m // tm,),
        in_specs=[row, row, vec, vec],
        out_specs=row,
        out_shape=jax.ShapeDtypeStruct((m, D_MODEL), F32),
        compiler_params=_cparams(("parallel",)),
        name="residual_ln",
    )(x, y, lg, lb)


def _moe(x16, x1, gate, w1p, b1g, b1l, w2, b2, lg, lb, tm_rows, alpha):
    m = x1.shape[0]
    d_ff = w2.shape[1]
    tm = min(MOE_ROWS, -(-m // LANES) * LANES)
    nt = -(-m // tm)
    pad = nt * tm - m
    x16 = jnp.pad(x16, ((0, pad), (0, 0)))
    gate = jnp.pad(gate, ((0, pad), (0, 0)))
    counts = jnp.sum((gate > 0.0).reshape(nt, tm, N_EXPERTS), axis=1).astype(I32).reshape(-1)
    gate_t = gate.T
    row = lambda w: pl.BlockSpec((tm, w), lambda i, e, c: (i, 0))
    grid_spec = pltpu.PrefetchScalarGridSpec(
        num_scalar_prefetch=1,
        grid=(nt, N_EXPERTS),
        in_specs=[row(D_MODEL), row(N_EXPERTS),
                  pl.BlockSpec((N_EXPERTS, tm), lambda i, e, c: (0, i)),
                  pl.BlockSpec((1, D_MODEL, d_ff), lambda i, e, c: (e, 0, 0)),
                  pl.BlockSpec((1, D_MODEL, d_ff), lambda i, e, c: (e, 0, 1)),
                  pl.BlockSpec((1, 1, d_ff), lambda i, e, c: (e, 0, 0)),
                  pl.BlockSpec((1, 1, d_ff), lambda i, e, c: (e, 0, 0)),
                  pl.BlockSpec((1, d_ff, D_MODEL), lambda i, e, c: (e, 0, 0)),
                  pl.BlockSpec((1, 1, D_MODEL), lambda i, e, c: (e, 0, 0))],
        out_specs=row(D_MODEL),
        scratch_shapes=[pltpu.VMEM((tm, N_EXPERTS), F32), pltpu.VMEM((N_EXPERTS, tm), F32)],
    )
    mixed = pl.pallas_call(
        functools.partial(_moe_kernel, tm=tm),
        grid_spec=grid_spec,
        out_shape=jax.ShapeDtypeStruct((nt * tm, D_MODEL), F32),
        compiler_params=_cparams(("parallel", "arbitrary")),
        name="moe_experts",
    )(counts, x16, gate, gate_t, w1p, w1p, b1g, b1l, w2, b2)
    return _residual_ln(x1, mixed, lg, lb, tm_rows, alpha)


def _deinterleave_w1(w1):
    f2 = w1.shape[-1]
    f = lax.broadcasted_iota(I32, (f2, f2), 0)
    g = lax.broadcasted_iota(I32, (f2, f2), 1)
    src = jnp.where(g < f2 // 2, 2 * g, 2 * (g - f2 // 2) + 1)
    perm = (f == src).astype(BF16)
    return jnp.einsum("edf,fg->edg", w1.astype(BF16), perm, preferred_element_type=BF16)


def _repack_w_in(w, b):
    offs = np.concatenate([[0], np.cumsum(PROJ_SIZES)])
    seg = lambda a, i: a[..., int(offs[i]):int(offs[i + 1])]
    d = w.shape[0]

    def build(a, rows):
        z = lambda n: jnp.zeros(rows + (n,), a.dtype)
        misc = jnp.concatenate([z(MISC_BF), seg(a, 6), seg(a, 12), z(LANES - MISC_IW - H_I)], axis=-1)
        main = jnp.concatenate([seg(a, 0), seg(a, 1), seg(a, 2), seg(a, 3), seg(a, 4), seg(a, 5),
                                seg(a, 7), seg(a, 8), seg(a, 9), seg(a, 10), seg(a, 11), seg(a, 11), misc], axis=-1)
        return main, seg(a, 13)

    wm, wg = build(w, (d,))
    bm, bg = build(b[None, :], (1,))
    return wm.astype(BF16), bm, wg.astype(BF16), bg


def _rope_tables(pos):
    half = ROT_DIM // 2
    inv = ROPE_THETA ** (-jnp.arange(half, dtype=F32) / half)
    ang = pos.astype(F32)[:, None] * inv
    cos, sin = jnp.cos(ang), jnp.sin(ang)
    m = pos.shape[0]
    ones = jnp.ones((m, DH - ROT_DIM), F32)
    zeros = jnp.zeros((m, DH - ROT_DIM), F32)
    z8 = jnp.zeros((m, half), F32)
    c64 = jnp.concatenate([cos, cos, ones], axis=1)
    a64 = jnp.concatenate([-sin, z8, zeros], axis=1)
    b64 = jnp.concatenate([z8, sin, zeros], axis=1)
    dup = lambda a: jnp.concatenate([a, a], axis=1)
    return dup(c64), dup(a64), dup(b64)


def kernel(x_prompt, x_sample, cache_a_k, cache_a_v, cache_b_k, cache_b_v, cache_b_logf, cache_c_k, cache_c_v,
           cache_c_idx_k, page_table, meta, w_in, b_in, a_lambda, a_norm_g, w_br_a, w_br_b, w_br_c, w_out,
           ln1_g, ln1_b, ln2_g, ln2_b, w_router, b_router, w_mlp1, b_mlp1, w_mlp2, b_mlp2):
    n_p, seq, _ = x_prompt.shape
    nseq, dec, _ = x_sample.shape
    depth = w_in.shape[0]
    pages = page_table.shape[1]
    past_len = pages * PAGE
    t = seq + BLOCK_Q
    topk_prompt = min(TOPK_MAX, seq // 4)
    topk_sample = min(TOPK_MAX, (past_len + dec) // 4)
    alpha = (2 * depth) ** 0.25
    mp = n_p * t
    m_all = mp + nseq * dec
    tm = _pick(m_all, (768, 640, 512, 384, 256, 128, 64, 32, 16, 8))

    xp = jnp.concatenate([jnp.zeros((n_p, META_PAD, D_MODEL), F32),
                          jnp.broadcast_to(meta[None], (n_p, N_META, D_MODEL)), x_prompt], axis=1)
    x = jnp.concatenate([xp.reshape(mp, D_MODEL), x_sample.reshape(nseq * dec, D_MODEL)], axis=0)
    pos = jnp.concatenate([jnp.tile(jnp.arange(t, dtype=I32) - META_PAD, n_p),
                           jnp.tile(past_len + jnp.arange(dec, dtype=I32), nseq)])
    cos, sa, sb = _rope_tables(pos)
    pt_flat = page_table.reshape(-1).astype(I32)
    flat_a = lambda c: c.reshape(c.shape[0], c.shape[1], PAGE * H_A, 2 * DH)
    key_minor = lambda c: jnp.transpose(c, (0, 1, 3, 4, 2))
    caches = (flat_a(cache_a_k), flat_a(cache_a_v), key_minor(cache_b_k), key_minor(cache_b_v),
              key_minor(cache_c_k), key_minor(cache_c_v))
    idx_kt = jnp.transpose(cache_c_idx_k, (0, 1, 3, 2))
    logf_t_cache = jnp.transpose(cache_b_logf, (0, 1, 3, 2))

    rows_p, rows_s = [], []
    for layer in range(depth):
        wm, bm, wg, bg = _repack_w_in(w_in[layer], b_in[layer])
        x16 = x.astype(BF16)
        p32, p16 = _project_main(x16, wm, bm, cos, sa, sb, tm)
        gates = _project_gates(x16, wg, bg, tm)
        lam = a_lambda[layer]
        g = a_norm_g[layer][None, :]

        bf_t = jnp.swapaxes(p32[:mp, MISC * LANES + MISC_BF:MISC * LANES + MISC_BF + H_B].reshape(n_p, t, H_B), 1, 2)
        logf_t, csum_t = _prompt_cumsum(bf_t)
        bias = _prompt_index(p16, p32, n_p, t, topk_prompt)
        oa = _prompt_flash("A", p16, (lam, g), n_p, t, layer)
        ob = _prompt_flash("B", p16, (jnp.swapaxes(csum_t, 1, 2), csum_t), n_p, t, layer)
        oc = _prompt_flash("C", p16, (bias,), n_p, t, layer)

        p32s = p32[mp:]
        sbias, scsum, scq, slogf = _sample_pre(pt_flat, p32s, idx_kt, logf_t_cache, layer, nseq, dec,
                                               pages, topk_sample)
        soa, sob, soc = _sample_attn(pt_flat, p32s, caches, scsum, scq, sbias, lam, g, layer, nseq, dec, pages)

        oa = jnp.concatenate([oa, soa.astype(BF16)], axis=0)
        ob = jnp.concatenate([ob, sob.astype(BF16)], axis=0)
        oc = jnp.concatenate([oc, soc.astype(BF16)], axis=0)
        x1, x1_16, gate = _merge(oa, ob, oc, gates, x,
                                 w_br_a[layer].astype(BF16), w_br_b[layer].astype(BF16), w_br_c[layer].astype(BF16),
                                 w_out[layer].astype(BF16), ln1_g[layer][None], ln1_b[layer][None],
                                 w_router[layer], b_router[layer][None], tm, alpha)
        x = _moe(x1_16, x1, gate, _deinterleave_w1(w_mlp1[layer]),
                 b_mlp1[layer][:, None, 0::2], b_mlp1[layer][:, None, 1::2],
                 w_mlp2[layer].astype(BF16), b_mlp2[layer][:, None, :],
                 ln2_g[layer][None], ln2_b[layer][None], tm, alpha)

        def seg(rows, col, width, shape):
            return rows[:, col * LANES:col * LANES + width].reshape(shape)

        tp = t - META_PAD
        p32p = p32[:mp].reshape(n_p, t, N_MAIN)

        def pseg(col, width, shape):
            return p32p[:, META_PAD:, col * LANES:col * LANES + width].reshape(shape)

        rows_p.append((pseg(AK, W_A, (n_p, tp, H_A, 2 * DH)), pseg(AV, W_A, (n_p, tp, H_A, 2 * DH)),
                       pseg(BK, W_B, (n_p, tp, H_B, DH)), pseg(BV, W_B, (n_p, tp, H_B, DH)),
                       jnp.swapaxes(logf_t, 1, 2)[:, META_PAD:],
                       pseg(CK, W_C, (n_p, tp, H_C, DH)), pseg(CV, W_C, (n_p, tp, H_C, DH)),
                       pseg(IK, D_IDX, (n_p, tp, D_IDX))))
        rows_s.append((seg(p32s, AK, W_A, (nseq, dec, H_A, 2 * DH)), seg(p32s, AV, W_A, (nseq, dec, H_A, 2 * DH)),
                       seg(p32s, BK, W_B, (nseq, dec, H_B, DH)), seg(p32s, BV, W_B, (nseq, dec, H_B, DH)),
                       slogf,
                       seg(p32s, CK, W_C, (nseq, dec, H_C, DH)), seg(p32s, CV, W_C, (nseq, dec, H_C, DH)),
                       seg(p32s, IK, D_IDX, (nseq, dec, D_IDX))))

    st = lambda rows, i: jnp.stack([r[i] for r in rows])
    y_prompt = x[:mp].reshape(n_p, t, D_MODEL)[:, BLOCK_Q:]
    y_sample = x[mp:].reshape(nseq, dec, D_MODEL)
    return (y_prompt, y_sample) + tuple(st(rows_p, i) for i in range(8)) + tuple(st(rows_s, i) for i in range(8))
```
